```python
import jax, jax.numpy as jnp
from jax import lax
import numpy as np

D_MODEL = 2048
BATCH = 16
SEQ = 256
DEPTH = 1
DEC_BATCH = 4
DEC_SEQ = 1024
PAST_LEN = 512

GRID_W = 64
HEAD_SIZE = 64
D_A = 2048
N_HEADS_A = D_A // HEAD_SIZE
D_B = 2048
R_DECAY = 96
R_ICLR = 96
N_DIR = 2
RMS_EPS = 1e-6
GN_EPS = HEAD_SIZE * 1e-5
N_SHIFTED = 3 * D_A + N_DIR * (R_DECAY + R_ICLR)
D_IN = N_SHIFTED + D_A + 4 * D_B + 2 * D_MODEL

kernel_name = "hybrid_rwkv7_shortconv_diffusion_step"


def _rmsnorm(x, g):
    xf = x.astype(jnp.float32)
    xf = xf * lax.rsqrt(jnp.mean(xf * xf, axis=-1, keepdims=True) + RMS_EPS)
    return (xf * g.astype(jnp.float32)).astype(x.dtype)


def _token_shift(z, mu_prev, mu_next):
    zp = jnp.pad(z[:, :-1], ((0, 0), (1, 0), (0, 0)))
    zn = jnp.pad(z[:, 1:], ((0, 0), (0, 1), (0, 0)))
    return z + mu_prev * (zp - z) + mu_next * (zn - z)


def _conv3_rows(u, w, rows):
    b, t, ch = u.shape
    ur = u.reshape(b, rows, t // rows, ch)
    up = jnp.pad(ur, ((0, 0), (0, 0), (1, 1), (0, 0)))
    out = w[0] * up[:, :, :-2] + w[1] * up[:, :, 1:-1] + w[2] * up[:, :, 2:]
    return out.reshape(b, t, ch)


def _heads(u):
    return u.reshape(u.shape[:-1] + (N_HEADS_A, HEAD_SIZE))


def _wkv_bidir(S0, r, decay, k, v, a_vec, b_vec):
    def orient(t):
        return jnp.stack([t[0], jnp.flip(t[1], axis=1)], axis=0)

    xs = tuple(jnp.moveaxis(orient(t), 2, 0) for t in (r, decay, k, v, a_vec, b_vec))

    def step(S, inp):
        r_t, w_t, k_t, v_t, a_t, b_t = inp
        sa = jnp.einsum('dbhij,dbhj->dbhi', S, a_t)
        S = S * w_t[..., None, :] + sa[..., :, None] * b_t[..., None, :] + v_t[..., :, None] * k_t[..., None, :]
        y = jnp.einsum('dbhij,dbhj->dbhi', S, r_t)
        return S, y

    S_fin, ys = lax.scan(step, S0, xs)
    return S_fin, orient(jnp.moveaxis(ys, 0, 2))


def _mixer(h, S0, rows, w_in, mu_prev, mu_next, w0, w2, a0, a2, k_k, k_a, r_k, lnx_g, lnx_b,
           conv_w, w_out_a, w_out_b, w_o):
    f32 = jnp.float32
    bsz, t, _ = h.shape
    z = h @ w_in
    zs = _token_shift(z[..., :N_SHIFTED], mu_prev, mu_next).astype(f32)
    r = zs[..., :D_A]
    k = zs[..., D_A:2 * D_A]
    v = zs[..., 2 * D_A:3 * D_A]
    o_w = 3 * D_A + N_DIR * R_DECAY
    wd = zs[..., 3 * D_A:o_w].reshape(bsz, t, N_DIR, R_DECAY)
    ad = zs[..., o_w:].reshape(bsz, t, N_DIR, R_ICLR)
    g_a, b_gate, c_gate, x_b, g_b, m_a, m_b = jnp.split(
        z[..., N_SHIFTED:],
        [D_A, D_A + D_B, D_A + 2 * D_B, D_A + 3 * D_B, D_A + 4 * D_B, D_A + 4 * D_B + D_MODEL], axis=-1)

    w_lin = w0[:, None, None, :] + jnp.einsum('btdr,drc->dbtc', jnp.tanh(wd), w2)
    decay = jnp.exp(-jnp.exp(-jax.nn.softplus(-w_lin) - 0.5))
    a = jax.nn.sigmoid(a0[:, None, None, :] + jnp.einsum('btdr,drc->dbtc', ad, a2))
    kk = _heads(k * k_k)
    kk = kk / jnp.maximum(jnp.sqrt(jnp.sum(kk * kk, axis=-1, keepdims=True)), 1e-12)
    kk = kk.reshape(bsz, t, D_A)
    k_dir = k[None] * (1.0 + (a - 1.0) * k_a)
    r2 = jnp.broadcast_to(r[None], (N_DIR,) + r.shape)
    v2 = jnp.broadcast_to(v[None], (N_DIR,) + v.shape)
    kk2 = jnp.broadcast_to(kk[None], (N_DIR,) + kk.shape)
    S_fin, ys = _wkv_bidir(S0, _heads(r2), _heads(decay), _heads(k_dir), _heads(v2),
                           _heads(-kk2), _heads(kk2 * a))
    y = ys[0] + ys[1]
    mu = jnp.mean(y, axis=-1, keepdims=True)
    var = jnp.mean(jnp.square(y - mu), axis=-1, keepdims=True)
    y = ((y - mu) * lax.rsqrt(var + GN_EPS)).reshape(bsz, t, D_A) * lnx_g + lnx_b
    bonus = jnp.sum(_heads(r2 * k_dir) * r_k, axis=-1, keepdims=True) * _heads(v2)
    y = y + jnp.sum(bonus, axis=0).reshape(bsz, t, D_A)
    y_a = (y.astype(h.dtype) * jax.nn.silu(g_a)) @ w_out_a

    u = _conv3_rows(c_gate * x_b, conv_w, rows)
    y_b = (b_gate * u * jax.nn.silu(g_b)) @ w_out_b

    merged = jax.nn.sigmoid(m_a) * y_a + jax.nn.sigmoid(m_b) * y_b
    return merged @ w_o, S_fin


def _block(x, mod, S0, rows, norm_g, *mixer_w):
    shift, scale, gate = jnp.split(mod, 3, axis=-1)
    h = _rmsnorm(x, norm_g) * (1.0 + scale) + shift
    out, S_fin = _mixer(h, S0, rows, *mixer_w)
    return x + gate * out, S_fin


def setup_inputs(seed: int = 0) -> dict:
    key = jax.random.key(seed)
    ks = jax.random.split(key, 26)
    f32 = jnp.float32

    def nrm(k, shape, s):
        return s * jax.random.normal(k, shape, f32)

    def uni(k, shape, lo, hi):
        return jax.random.uniform(k, shape, f32, lo, hi)

    L, H, N = DEPTH, N_HEADS_A, HEAD_SIZE
    return {
        "x_prompt": nrm(ks[0], (BATCH, SEQ, D_MODEL), 1.0),
        "x_sample": nrm(ks[1], (DEC_BATCH, DEC_SEQ, D_MODEL), 1.0),
        "state_wkv_fwd": nrm(ks[2], (DEC_BATCH, L, H, N, N), 0.5),
        "state_wkv_bwd": nrm(ks[3], (DEC_BATCH, L, H, N, N), 0.5),
        "c": nrm(ks[4], (DEC_BATCH, D_MODEL), 1.0),
        "c_ctx": nrm(ks[5], (D_MODEL,), 1.0),
        "ada_w": nrm(ks[6], (L, D_MODEL, 3 * D_MODEL), 0.5 * D_MODEL ** -0.5),
        "ada_b": nrm(ks[7], (L, 3 * D_MODEL), 0.02),
        "norm_g": 1.0 + nrm(ks[8], (L, D_MODEL), 0.02),
        "w_in": nrm(ks[9], (L, D_MODEL, D_IN), D_MODEL ** -0.5),
        "mu_prev": uni(ks[10], (L, N_SHIFTED), 0.0, 0.5),
        "mu_next": uni(ks[11], (L, N_SHIFTED), 0.0, 0.5),
        "w0": uni(ks[12], (L, N_DIR, D_A), -4.0, -0.5),
        "w2": nrm(ks[13], (L, N_DIR, R_DECAY, D_A), 0.3 * R_DECAY ** -0.5),
        "a0": nrm(ks[14], (L, N_DIR, D_A), 0.1),
        "a2": nrm(ks[15], (L, N_DIR, R_ICLR, D_A), R_ICLR ** -0.5),
        "k_k": 0.85 + nrm(ks[16], (L, D_A), 0.05),
        "k_a": 1.0 + nrm(ks[17], (L, D_A), 0.05),
        "r_k": nrm(ks[18], (L, H, N), 0.1),
        "lnx_g": 1.0 + nrm(ks[19], (L, D_A), 0.02),
        "lnx_b": nrm(ks[20], (L, D_A), 0.01),
        "conv_w": nrm(ks[21], (L, 3, D_B), 3 ** -0.5),
        "w_out_a": nrm(ks[22], (L, D_A, D_MODEL), D_A ** -0.5),
        "w_out_b": nrm(ks[23], (L, D_B, D_MODEL), D_B ** -0.5),
        "w_o": nrm(ks[24], (L, D_MODEL, D_MODEL), D_MODEL ** -0.5),
        "final_g": 1.0 + nrm(ks[25], (D_MODEL,), 0.02),
    }


def reference(x_prompt, x_sample, state_wkv_fwd, state_wkv_bwd, c, c_ctx, ada_w, ada_b, norm_g, w_in,
              mu_prev, mu_next, w0, w2, a0, a2, k_k, k_a, r_k, lnx_g, lnx_b, conv_w, w_out_a, w_out_b,
              w_o, final_g):
    f32 = jnp.float32
    n_ctx_batch = x_prompt.shape[0]
    lat_rows = x_sample.shape[1] // GRID_W
    xp = x_prompt
    xl = x_sample
    fwd_states = []
    bwd_states = []
    for l in range(DEPTH):
        lw = (norm_g[l], w_in[l], mu_prev[l], mu_next[l], w0[l], w2[l], a0[l], a2[l], k_k[l], k_a[l],
              r_k[l], lnx_g[l], lnx_b[l], conv_w[l], w_out_a[l], w_out_b[l], w_o[l])
        mod_ctx = (jax.nn.silu(c_ctx) @ ada_w[l] + ada_b[l])[None, None, :]
        mod_lat = (jax.nn.silu(c) @ ada_w[l] + ada_b[l])[:, None, :]
        S0_ctx = jnp.zeros((N_DIR, n_ctx_batch, N_HEADS_A, HEAD_SIZE, HEAD_SIZE), f32)
        xp, S_ctx = _block(xp, mod_ctx, S0_ctx, 1, *lw)
        fwd_states.append(S_ctx[0])
        bwd_states.append(S_ctx[1])
        S0_lat = jnp.stack([state_wkv_fwd[:, l], state_wkv_bwd[:, l]], axis=0).astype(f32)
        xl, _ = _block(xl, mod_lat, S0_lat, lat_rows, *lw)
    y_prompt = _rmsnorm(xp, final_g)
    y_sample = _rmsnorm(xl, final_g)
    new_state_wkv_fwd = jnp.stack(fwd_states, axis=1).astype(x_prompt.dtype)
    new_state_wkv_bwd = jnp.stack(bwd_states, axis=1).astype(x_prompt.dtype)
    return (y_prompt, y_sample, new_state_wkv_fwd, new_state_wkv_bwd)
```

```python
import functools

import jax
import jax.numpy as jnp
from jax import lax
from jax.experimental import pallas as pl
from jax.experimental.pallas import tpu as pltpu

F32 = jnp.float32
BF16 = jnp.bfloat16

D_MODEL = 2048
D_A = 2048
D_B = 2048
HEAD = 64
N_HEADS = D_A // HEAD
R_LORA = 96
R_PAD = 128
N_DIR = 2
GRID_W = 64
RMS_EPS = 1e-6
GN_EPS = HEAD * 1e-5
N_RKV = 3 * D_A
N_LORA = 2 * N_DIR * R_LORA
N_SHIFTED = N_RKV + N_LORA
N_REST = D_A + 4 * D_B + 2 * D_MODEL

CHUNK = 64
INV_BASE = 4
G_HEADS = 4
GC = G_HEADS * HEAD
N_GROUPS = N_HEADS // G_HEADS

V7X_VMEM_LIMIT = 60 * 1024 * 1024


def _cparams(sem):
    return pltpu.CompilerParams(dimension_semantics=sem, vmem_limit_bytes=V7X_VMEM_LIMIT)


def _bdot(a, b):
    return jnp.dot(a.astype(BF16), b.astype(BF16), preferred_element_type=F32)


def _bdot_nt(a, b):
    return lax.dot_general(a.astype(BF16), b.astype(BF16), (((1,), (1,)), ((), ())),
                           preferred_element_type=F32)


def _bdot_tn(a, b):
    return lax.dot_general(a.astype(BF16), b.astype(BF16), (((0,), (0,)), ((), ())),
                           preferred_element_type=F32)


def _split3(x):
    h1 = x.astype(BF16)
    r1 = x - h1.astype(F32)
    h2 = r1.astype(BF16)
    h3 = (r1 - h2.astype(F32)).astype(BF16)
    return h1, h2, h3


def _dot_exact_rhs(x, m16):
    h1, h2, h3 = _split3(x)
    d = lambda a: jnp.dot(a, m16, preferred_element_type=F32)
    return d(h1) + d(h2) + d(h3)


def _sigmoid(x):
    return 1.0 / (1.0 + jnp.exp(-x))


def _softplus(x):
    return jnp.maximum(x, 0.0) + jnp.log(1.0 + jnp.exp(-jnp.abs(x)))


def _ada_kernel(c_ref, w_ref, b_ref, o_ref):
    c = c_ref[...]
    s = c * _sigmoid(c)
    o_ref[...] = jnp.dot(s, w_ref[...], precision=lax.Precision.HIGHEST,
                         preferred_element_type=F32) + b_ref[...]


def _ada_call(cvec, ada_w, ada_b):
    n = ada_w.shape[1]
    tn = 1024
    return pl.pallas_call(
        _ada_kernel,
        grid=(n // tn,),
        in_specs=[pl.BlockSpec((8, D_MODEL), lambda j: (0, 0)),
                  pl.BlockSpec((D_MODEL, tn), lambda j: (0, j)),
                  pl.BlockSpec((1, tn), lambda j: (0, j))],
        out_specs=pl.BlockSpec((8, tn), lambda j: (0, j)),
        out_shape=jax.ShapeDtypeStruct((8, n), F32),
        compiler_params=_cparams(("arbitrary",)),
        name="ada",
    )(cvec, ada_w, ada_b)


IN_TM = 1024
NORM_ROWS = 128


def _inproj_kernel(*refs, seq_len, do_shift):
    if do_shift:
        x_ref, sh_ref, sc_ref, g_ref, w_ref, mup_ref, mun_ref, o_ref, h_ref = refs
    else:
        x_ref, sh_ref, sc_ref, g_ref, w_ref, o_ref, h_ref = refs

    @pl.when(pl.program_id(1) == 0)
    def _():
        g = g_ref[...]
        sc = 1.0 + sc_ref[...]
        sh = sh_ref[...]

        def body(i, carry):
            rows = pl.ds(pl.multiple_of(i * NORM_ROWS, NORM_ROWS), NORM_ROWS)
            x = x_ref[rows, :]
            ms = jnp.mean(x * x, axis=-1, keepdims=True)
            xn = x * lax.rsqrt(ms + RMS_EPS) * g
            h_ref[rows, :] = (xn * sc + sh).astype(BF16)
            return carry

        lax.fori_loop(0, IN_TM // NORM_ROWS, body, 0)

    z = jnp.dot(h_ref[...], w_ref[...].astype(BF16), preferred_element_type=F32)
    if do_shift:
        pos = lax.broadcasted_iota(jnp.int32, z.shape, 0) & (seq_len - 1)
        zp = jnp.where(pos == 0, 0.0, pltpu.roll(z, 1, 0))
        zn = jnp.where(pos == seq_len - 1, 0.0, pltpu.roll(z, IN_TM - 1, 0))
        z = z + mup_ref[...] * (zp - z) + mun_ref[...] * (zn - z)
    o_ref[...] = z


def _inproj_call(x, mod, norm_g, w, col0, ncols, tn, seq_len, mod_row0, mod_row_step, mu=None, name="inproj"):
    m = x.shape[0]
    assert m % IN_TM == 0 and IN_TM % seq_len == 0 and ncols % tn == 0 and col0 % 128 == 0
    do_shift = mu is not None
    row = lambda i: mod_row0 + mod_row_step * i
    in_specs = [
        pl.BlockSpec((IN_TM, D_MODEL), lambda i, j: (i, 0)),
        pl.BlockSpec((None, 1, D_MODEL), lambda i, j: (row(i), 0, 0)),
        pl.BlockSpec((None, 1, D_MODEL), lambda i, j: (row(i), 0, 1)),
        pl.BlockSpec((1, D_MODEL), lambda i, j: (0, 0)),
    ]
    if col0 % tn == 0:
        in_specs.append(pl.BlockSpec((D_MODEL, tn), lambda i, j: (0, col0 // tn + j)))
    else:
        in_specs.append(pl.BlockSpec((pl.Element(D_MODEL), pl.Element(tn)), lambda i, j: (0, pl.multiple_of(col0 + j * tn, 128))))
    args = [x, mod, mod, norm_g, w]
    if do_shift:
        mup, mun = mu
        in_specs += [pl.BlockSpec((1, tn), lambda i, j: (0, j))] * 2
        args += [mup, mun]
    return pl.pallas_call(
        functools.partial(_inproj_kernel, seq_len=seq_len, do_shift=do_shift),
        grid=(m // IN_TM, ncols // tn),
        in_specs=in_specs,
        out_specs=pl.BlockSpec((IN_TM, tn), lambda i, j: (i, j)),
        out_shape=jax.ShapeDtypeStruct((m, ncols), F32),
        scratch_shapes=[pltpu.VMEM((IN_TM, D_MODEL), BF16)],
        compiler_params=_cparams(("arbitrary", "arbitrary")),
        name=name,
    )(*args)


def _bd16(x):
    t = jnp.concatenate([x] * G_HEADS, axis=0)
    rb = lax.broadcasted_iota(jnp.int32, (GC, GC), 0) // HEAD
    cb = lax.broadcasted_iota(jnp.int32, (GC, GC), 1) // HEAD
    return jnp.where(rb == cb, t, 0.0).astype(BF16)


def _wkv_kernel(*refs, seq_len, has_s0, emit_state):
    it = iter(refs)
    r_ref, k_ref, v_ref, lora_ref, ga_ref, p8_ref, w0a0_ref, w2a2_ref = (next(it) for _ in range(8))
    s0_refs = (next(it), next(it)) if has_s0 else None
    ya_ref = next(it)
    st_refs = (next(it), next(it)) if emit_state else None
    (at_ref, rt_ref, arb_ref, bt_ref, kt_ref, u0_ref, y0_ref, pc_ref,
     yacc_ref, bonus_ref, s_ref) = (next(it) for _ in range(11))

    n_chunks = seq_len // CHUNK
    p8 = p8_ref[...]
    k_k, k_a, r_k, lnx_g, lnx_b = (p8[i:i + 1] for i in range(5))
    w0a0 = w0a0_ref[...]

    row_c = lax.broadcasted_iota(jnp.int32, (CHUNK, GC), 0)
    col_c = lax.broadcasted_iota(jnp.int32, (CHUNK, GC), 1) & (HEAD - 1)
    strict = (col_c < row_c, col_c > row_c)
    incl = (col_c <= row_c, col_c >= row_c)
    eye_pk = jnp.where(col_c == row_c, 1.0, 0.0)
    same_blk = {}
    m = INV_BASE
    while m <= CHUNK:
        same_blk[m] = (row_c // m) == (col_c // m)
        m *= 2
    tr = lax.broadcasted_iota(jnp.int32, (CHUNK, CHUNK), 0)
    tc = lax.broadcasted_iota(jnp.int32, (CHUNK, CHUNK), 1)
    tri16 = (jnp.where(tc <= tr, 1.0, 0.0).astype(BF16), jnp.where(tc >= tr, 1.0, 0.0).astype(BF16))
    rb = lax.broadcasted_iota(jnp.int32, (GC, GC), 0) // HEAD
    cb = lax.broadcasted_iota(jnp.int32, (GC, GC), 1) // HEAD
    bd_mask = rb == cb
    ones_bd16 = jnp.where(bd_mask, 1.0, 0.0).astype(BF16)

    def seg_sum(x):
        h1 = x.astype(BF16)
        h2 = (x - h1.astype(F32)).astype(BF16)
        return (jnp.dot(h1, ones_bd16, preferred_element_type=F32)
                + jnp.dot(h2, ones_bd16, preferred_element_type=F32))

    yacc_ref[...] = jnp.zeros_like(yacc_ref)

    def phase_a(c, carry):
        rows = pl.ds(pl.multiple_of(c * CHUNK, CHUNK), CHUNK)
        r = r_ref[rows, :]
        k = k_ref[rows, :]
        v = v_ref[rows, :]
        lo = lora_ref[rows, :]
        kk = k * k_k
        kk = kk / jnp.maximum(jnp.sqrt(seg_sum(kk * kk)), 1e-12)
        lw, kd, bb = [], [], []
        for d in range(N_DIR):
            wd = jnp.tanh(lo[:, d * R_PAD:(d + 1) * R_PAD])
            ad = lo[:, (N_DIR + d) * R_PAD:(N_DIR + d + 1) * R_PAD]
            w_lin = w0a0[d:d + 1] + _bdot(wd, w2a2_ref[d])
            lw.append(-jnp.exp(-_softplus(-w_lin) - 0.5))
            a = _sigmoid(w0a0[N_DIR + d:N_DIR + d + 1] + _bdot(ad, w2a2_ref[N_DIR + d]))
            kd.append(k * (1.0 + (a - 1.0) * k_a))
            bb.append(kk * a)
        bonus_ref[rows, :] = seg_sum(r * (kd[0] + kd[1]) * r_k) * v
        bd_v = _bd16(v)
        for d in range(N_DIR):
            lw_hi = lw[d].astype(BF16)
            lw_lo = (lw[d] - lw_hi.astype(F32)).astype(BF16)
            cs = (jnp.dot(tri16[d], lw_hi, preferred_element_type=F32)
                  + jnp.dot(tri16[d], lw_lo, preferred_element_type=F32))
            e = jnp.exp(cs)
            e_inv = jnp.exp(-cs)
            rt = r * e
            at = -kk * jnp.exp(cs - lw[d])
            bt = bb[d] * e_inv
            kt = kd[d] * e_inv
            bd_b = _bd16(bt)
            bd_k = _bd16(kt)
            l_ab = jnp.where(strict[d], _bdot_nt(at, bd_b), 0.0)
            l_ak = jnp.where(strict[d], _bdot_nt(at, bd_k), 0.0)
            a_rb = jnp.where(incl[d], _bdot_nt(rt, bd_b), 0.0)
            a_rk = jnp.where(incl[d], _bdot_nt(rt, bd_k), 0.0)
            l_base = jnp.where(same_blk[INV_BASE], l_ab, 0.0)
            tm = eye_pk + l_base
            tm = tm + _bdot(tm, _bd16(_bdot(l_base, _bd16(l_base))))
            m = INV_BASE
            while m < CHUNK:
                l_off = jnp.where(same_blk[2 * m] & jnp.logical_not(same_blk[m]), l_ab, 0.0)
                tm = tm + _bdot(_bdot(tm, _bd16(l_off)), _bd16(tm))
                m *= 2
            at_ref[d, c] = _bdot(tm, _bd16(at)).astype(BF16)
            u0_ref[d, c] = _bdot(tm, _bd16(_bdot(l_ak, bd_v)))
            y0_ref[d, c] = _bdot(a_rk, bd_v)
            rt_ref[d, c] = rt.astype(BF16)
            arb_ref[d, c] = a_rb.astype(BF16)
            bt_ref[d, c] = bt.astype(BF16)
            kt_ref[d, c] = kt.astype(BF16)
            edge = e[CHUNK - 1:CHUNK] if d == 0 else e[0:1]
            pc_ref[d, c] = jnp.broadcast_to(edge, (8, GC))
        return carry

    lax.fori_loop(0, n_chunks, phase_a, 0)

    if has_s0:
        tile16 = jnp.where(lax.broadcasted_iota(jnp.int32, (HEAD, GC), 0)
                           == (lax.broadcasted_iota(jnp.int32, (HEAD, GC), 1) & (HEAD - 1)), 1.0, 0.0).astype(BF16)
        for d in range(N_DIR):
            s_ref[d] = jnp.where(bd_mask, _dot_exact_rhs(s0_refs[d][0], tile16), 0.0)
    else:
        s_ref[...] = jnp.zeros_like(s_ref)

    def phase_b(i, carry):
        for d in range(N_DIR):
            c = i if d == 0 else n_chunks - 1 - i
            rows = pl.ds(pl.multiple_of(c * CHUNK, CHUNK), CHUNK)
            s = s_ref[d]
            s16 = s.astype(BF16)
            u = _bdot_nt(at_ref[d, c], s16) + u0_ref[d, c]
            y = _bdot_nt(rt_ref[d, c], s16) + _bdot(arb_ref[d, c], _bd16(u)) + y0_ref[d, c]
            yacc_ref[rows, :] += y
            uv = jnp.concatenate([u.astype(BF16), v_ref[rows, :].astype(BF16)], axis=0)
            bk = jnp.concatenate([bt_ref[d, c], kt_ref[d, c]], axis=0)
            ds = _bdot_tn(uv, bk)
            s_ref[d] = (s + jnp.where(bd_mask, ds, 0.0)) * pc_ref[d, c][0:1]
        return carry

    lax.fori_loop(0, n_chunks, phase_b, 0)

    if emit_state:
        untile16 = jnp.where((lax.broadcasted_iota(jnp.int32, (GC, HEAD), 0) & (HEAD - 1))
                             == lax.broadcasted_iota(jnp.int32, (GC, HEAD), 1), 1.0, 0.0).astype(BF16)
        for d in range(N_DIR):
            st_refs[d][0] = _dot_exact_rhs(s_ref[d], untile16)

    def phase_c(c, carry):
        rows = pl.ds(pl.multiple_of(c * CHUNK, CHUNK), CHUNK)
        y = yacc_ref[rows, :]
        mu = seg_sum(y) * (1.0 / HEAD)
        yc = y - mu
        var = seg_sum(yc * yc) * (1.0 / HEAD)
        out = yc * lax.rsqrt(var + GN_EPS) * lnx_g + lnx_b + bonus_ref[rows, :]
        ga = ga_ref[rows, :]
        ya_ref[rows, :] = (out * (ga * _sigmoid(ga))).astype(BF16)
        return carry

    lax.fori_loop(0, n_chunks, phase_c, 0)


def _wkv_call(z_rkv, z_lora, z_rest, p8, w0a0, w2a2, seq_len, s0=None, emit_state=False, name="wkv"):
    m = z_rkv.shape[0]
    nb = m // seq_len
    nc = seq_len // CHUNK
    has_s0 = s0 is not None
    ncb = D_A // GC
    in_specs = [
        pl.BlockSpec((seq_len, GC), lambda b, g: (b, g)),
        pl.BlockSpec((seq_len, GC), lambda b, g: (b, ncb + g)),
        pl.BlockSpec((seq_len, GC), lambda b, g: (b, 2 * ncb + g)),
        pl.BlockSpec((seq_len, 2 * N_DIR * R_PAD), lambda b, g: (b, 0)),
        pl.BlockSpec((seq_len, GC), lambda b, g: (b, g)),
        pl.BlockSpec((8, GC), lambda b, g: (0, g)),
        pl.BlockSpec((2 * N_DIR, GC), lambda b, g: (0, g)),
        pl.BlockSpec((2 * N_DIR, R_PAD, GC), lambda b, g: (0, 0, g)),
    ]
    args = [z_rkv, z_rkv, z_rkv, z_lora, z_rest, p8, w0a0, w2a2]
    if has_s0:
        in_specs += [pl.BlockSpec((1, GC, HEAD), lambda b, g: (b, g, 0))] * 2
        args += list(s0)
    out_specs = [pl.BlockSpec((seq_len, GC), lambda b, g: (b, g))]
    out_shape = [jax.ShapeDtypeStruct((m, D_A), BF16)]
    if emit_state:
        out_specs += [pl.BlockSpec((1, GC, HEAD), lambda b, g: (b, g, 0))] * 2
        out_shape += [jax.ShapeDtypeStruct((nb, D_A, HEAD), F32)] * 2
    big16 = pltpu.VMEM((N_DIR, nc, CHUNK, GC), BF16)
    big32 = pltpu.VMEM((N_DIR, nc, CHUNK, GC), F32)
    scratch = [big16, big16, big16, big16, big16, big32, big32,
               pltpu.VMEM((N_DIR, nc, 8, GC), F32),
               pltpu.VMEM((seq_len, GC), F32), pltpu.VMEM((seq_len, GC), F32),
               pltpu.VMEM((N_DIR, GC, GC), F32)]
    return pl.pallas_call(
        functools.partial(_wkv_kernel, seq_len=seq_len, has_s0=has_s0, emit_state=emit_state),
        grid=(nb, N_GROUPS),
        in_specs=in_specs,
        out_specs=out_specs,
        out_shape=out_shape,
        scratch_shapes=scratch,
        compiler_params=_cparams(("arbitrary", "arbitrary")),
        name=name,
    )(*args)


CONV_TM = 256
CONV_TN = 1024


def _conv_kernel(b_ref, c_ref, x_ref, g_ref, w_ref, o_ref, *, row_len):
    p = c_ref[...] * x_ref[...]
    pos = lax.broadcasted_iota(jnp.int32, p.shape, 0) & (row_len - 1)
    pp = jnp.where(pos == 0, 0.0, pltpu.roll(p, 1, 0))
    pn = jnp.where(pos == row_len - 1, 0.0, pltpu.roll(p, CONV_TM - 1, 0))
    w = w_ref[...]
    u = w[0:1] * pp + w[1:2] * p + w[2:3] * pn
    g = g_ref[...]
    o_ref[...] = (b_ref[...] * u * (g * _sigmoid(g))).astype(BF16)


def _conv_call(z_rest, conv_w, row_len, name="conv"):
    m = z_rest.shape[0]
    nb = D_B // CONV_TN
    off = D_A // CONV_TN
    spec = lambda q: pl.BlockSpec((CONV_TM, CONV_TN), lambda i, j: (i, off + q * nb + j))
    return pl.pallas_call(
        functools.partial(_conv_kernel, row_len=row_len),
        grid=(m // CONV_TM, nb),
        in_specs=[spec(0), spec(1), spec(2), spec(3),
                  pl.BlockSpec((3, CONV_TN), lambda i, j: (0, j))],
        out_specs=pl.BlockSpec((CONV_TM, CONV_TN), lambda i, j: (i, j)),
        out_shape=jax.ShapeDtypeStruct((m, D_B), BF16),
        compiler_params=_cparams(("arbitrary", "arbitrary")),
        name=name,
    )(z_rest, z_rest, z_rest, z_rest, conv_w)


OUT_TM = 256


def _outproj_kernel(ya_ref, yb_ref, ma_ref, mb_ref, x_ref, gate_ref, fg_ref, wa_ref, wb_ref, wo_ref, o_ref):
    y_a = jnp.dot(ya_ref[...], wa_ref[...], preferred_element_type=F32)
    y_b = jnp.dot(yb_ref[...], wb_ref[...], preferred_element_type=F32)
    merged = _sigmoid(ma_ref[...]) * y_a + _sigmoid(mb_ref[...]) * y_b
    out = jnp.dot(merged.astype(BF16), wo_ref[...], preferred_element_type=F32)
    xo = x_ref[...] + gate_ref[...] * out
    ms = jnp.mean(xo * xo, axis=-1, keepdims=True)
    o_ref[...] = xo * lax.rsqrt(ms + RMS_EPS) * fg_ref[...]


def _outproj_call(ya, yb, z_rest, x, mod, final_g, wa16, wb16, wo16, rows_per_mod, mod_row0, mod_row_step,
                  name="outproj"):
    m = x.shape[0]
    mcol = (D_A + 4 * D_B) // D_MODEL
    row = lambda i: mod_row0 + mod_row_step * (i // rows_per_mod)
    full = lambda: pl.BlockSpec((D_MODEL, D_MODEL), lambda i: (0, 0), pipeline_mode=pl.Buffered(1))
    tile = lambda col: pl.BlockSpec((OUT_TM, D_MODEL), lambda i: (i, col))
    return pl.pallas_call(
        _outproj_kernel,
        grid=(m // OUT_TM,),
        in_specs=[tile(0), tile(0), tile(mcol), tile(mcol + 1), tile(0),
                  pl.BlockSpec((None, 1, D_MODEL), lambda i: (row(i), 0, 2)),
                  pl.BlockSpec((1, D_MODEL), lambda i: (0, 0)),
                  full(), full(), full()],
        out_specs=tile(0),
        out_shape=jax.ShapeDtypeStruct((m, D_MODEL), F32),
        compiler_params=_cparams(("arbitrary",)),
        name=name,
    )(ya, yb, z_rest, z_rest, x, mod, final_g, wa16, wb16, wo16)


def _pad_lora(a, axis):
    shp = a.shape
    a = a.reshape(shp[:axis] + (2 * N_DIR, R_LORA) + shp[axis + 1:])
    pad = [(0, 0)] * a.ndim
    pad[axis + 1] = (0, R_PAD - R_LORA)
    a = jnp.pad(a, pad)
    return a.reshape(shp[:axis] + (2 * N_DIR * R_PAD,) + shp[axis + 1:])


def _stream(x3, mod, mod_row0, mod_row_step, conv_rows, lw, s0, emit_state, tag):
    (norm_g, w_in, w_lora, mu_rkv, mu_lora, p8, w0a0, w2a2, conv_w, wa16, wb16, wo16, final_g) = lw
    nb, seq_len, _ = x3.shape
    x = x3.reshape(nb * seq_len, D_MODEL)
    z_rkv = _inproj_call(x, mod, norm_g, w_in, 0, N_RKV, 512, seq_len, mod_row0, mod_row_step,
                         mu=mu_rkv, name="inproj_rkv_" + tag)
    z_lora = _inproj_call(x, mod, norm_g, w_lora, 0, 2 * N_DIR * R_PAD, 2 * N_DIR * R_PAD, seq_len,
                          mod_row0, mod_row_step, mu=mu_lora, name="inproj_lora_" + tag)
    z_rest = _inproj_call(x, mod, norm_g, w_in, N_SHIFTED, N_REST, 512, seq_len, mod_row0, mod_row_step,
                          name="inproj_rest_" + tag)
    wkv_out = _wkv_call(z_rkv, z_lora, z_rest, p8, w0a0, w2a2, seq_len, s0=s0, emit_state=emit_state,
                        name="wkv_" + tag)
    ya = wkv_out[0]
    yb = _conv_call(z_rest, conv_w, seq_len // conv_rows, name="conv_" + tag)
    y = _outproj_call(ya, yb, z_rest, x, mod, final_g, wa16, wb16, wo16, seq_len // OUT_TM,
                      mod_row0, mod_row_step, name="outproj_" + tag)
    return y.reshape(nb, seq_len, D_MODEL), wkv_out[1:]


def kernel(x_prompt, x_sample, state_wkv_fwd, state_wkv_bwd, c, c_ctx, ada_w, ada_b, norm_g, w_in, mu_prev,
           mu_next, w0, w2, a0, a2, k_k, k_a, r_k, lnx_g, lnx_b, conv_w, w_out_a, w_out_b, w_o, final_g):
    depth = norm_g.shape[0]
    assert depth == 1
    l = 0
    nbp = x_prompt.shape[0]
    nbs, seq_s, _ = x_sample.shape
    lat_rows = seq_s // GRID_W

    cvec = jnp.concatenate([c, c_ctx[None, :], jnp.zeros((8 - nbs - 1, D_MODEL), F32)], axis=0)
    mod = _ada_call(cvec, ada_w[l], ada_b[l][None, :]).reshape(8, 1, 3 * D_MODEL)

    w_lora = _pad_lora(w_in[l][:, N_RKV:N_SHIFTED], 1)
    mu_rkv = (mu_prev[l][None, :], mu_next[l][None, :])
    mu_lora = (_pad_lora(mu_prev[l][None, N_RKV:], 1), _pad_lora(mu_next[l][None, N_RKV:], 1))
    zrow = jnp.zeros((D_A,), F32)
    p8 = jnp.stack([k_k[l], k_a[l], r_k[l].reshape(D_A), lnx_g[l], lnx_b[l], zrow, zrow, zrow], axis=0)
    w0a0 = jnp.concatenate([w0[l], a0[l]], axis=0)
    w2a2 = jnp.pad(jnp.concatenate([w2[l], a2[l]], axis=0), ((0, 0), (0, R_PAD - R_LORA), (0, 0)))
    lw = (norm_g[l][None, :], w_in[l], w_lora, mu_rkv, mu_lora, p8, w0a0, w2a2, conv_w[l],
          w_out_a[l].astype(BF16), w_out_b[l].astype(BF16), w_o[l].astype(BF16), final_g[None, :])

    y_prompt, st = _stream(x_prompt, mod, nbs, 0, 1, lw, None, True, "ctx")
    s0 = (state_wkv_fwd[:, l].reshape(nbs, D_A, HEAD), state_wkv_bwd[:, l].reshape(nbs, D_A, HEAD))
    y_sample, _ = _stream(x_sample, mod, 0, 1, lat_rows, lw, s0, False, "lat")
    new_f = st[0].reshape(nbp, 1, N_HEADS, HEAD, HEAD)
    new_b = st[1].reshape(nbp, 1, N_HEADS, HEAD, HEAD)
    return (y_prompt, y_sample, new_f, new_b)
```

```python
import functools

import jax
import jax.numpy as jnp
from jax import lax
from jax.experimental import pallas as pl
from jax.experimental.pallas import tpu as pltpu

F32 = jnp.float32
BF16 = jnp.bfloat16

D_MODEL = 2048
D_A = 2048
D_B = 2048
HEAD = 64
N_HEADS = D_A // HEAD
R_LORA = 96
R_PAD = 128
N_DIR = 2
GRID_W = 64
RMS_EPS = 1e-6
GN_EPS = HEAD * 1e-5
N_RKV = 3 * D_A
N_LORA = 2 * N_DIR * R_LORA
N_SHIFTED = N_RKV + N_LORA
N_REST = D_A + 4 * D_B + 2 * D_MODEL

CHUNK = 64
INV_BASE = 4
G_HEADS = 4
GC = G_HEADS * HEAD
N_GROUPS = N_HEADS // G_HEADS
WKV_ROWS = 256
A_CHUNKS = 4

V7X_VMEM_LIMIT = 60 * 1024 * 1024


def _cparams(sem):
    return pltpu.CompilerParams(dimension_semantics=sem, vmem_limit_bytes=V7X_VMEM_LIMIT)


def _bdot(a, b):
    return jnp.dot(a.astype(BF16), b.astype(BF16), preferred_element_type=F32)


def _bdot_nt(a, b):
    return lax.dot_general(a.astype(BF16), b.astype(BF16), (((1,), (1,)), ((), ())),
                           preferred_element_type=F32)


def _bdot_tn(a, b):
    return lax.dot_general(a.astype(BF16), b.astype(BF16), (((0,), (0,)), ((), ())),
                           preferred_element_type=F32)


def _split3(x):
    h1 = x.astype(BF16)
    r1 = x - h1.astype(F32)
    h2 = r1.astype(BF16)
    h3 = (r1 - h2.astype(F32)).astype(BF16)
    return h1, h2, h3


def _dot_exact_rhs(x, m16):
    h1, h2, h3 = _split3(x)
    d = lambda a: jnp.dot(a, m16, preferred_element_type=F32)
    return d(h1) + d(h2) + d(h3)


def _sigmoid(x):
    return 1.0 / (1.0 + jnp.exp(-x))


EXP_M05 = 0.6065306597126334


def _ada_kernel(c_ref, w_ref, b_ref, o_ref):
    c = c_ref[...]
    s = c * _sigmoid(c)
    o_ref[...] = jnp.dot(s, w_ref[...], precision=lax.Precision.HIGHEST,
                         preferred_element_type=F32) + b_ref[...]


def _ada_call(cvec, ada_w, ada_b):
    n = ada_w.shape[1]
    tn = 1024
    return pl.pallas_call(
        _ada_kernel,
        grid=(n // tn,),
        in_specs=[pl.BlockSpec((8, D_MODEL), lambda j: (0, 0)),
                  pl.BlockSpec((D_MODEL, tn), lambda j: (0, j)),
                  pl.BlockSpec((1, tn), lambda j: (0, j))],
        out_specs=pl.BlockSpec((8, tn), lambda j: (0, j)),
        out_shape=jax.ShapeDtypeStruct((8, n), F32),
        compiler_params=_cparams(("arbitrary",)),
        name="ada",
    )(cvec, ada_w, ada_b)


NORM_TM = 512
NORM_ROWS = 128


def _norm_kernel(xc_ref, xl_ref, sh_ref, sc_ref, g_ref, o_ref, *, n_ctx_tiles):
    g = g_ref[...]
    sc = 1.0 + sc_ref[...]
    sh = sh_ref[...]

    def run(x_ref):
        def body(i, carry):
            rows = pl.ds(pl.multiple_of(i * NORM_ROWS, NORM_ROWS), NORM_ROWS)
            x = x_ref[rows, :]
            ms = jnp.mean(x * x, axis=-1, keepdims=True)
            xn = x * lax.rsqrt(ms + RMS_EPS) * g
            o_ref[rows, :] = (xn * sc + sh).astype(BF16)
            return carry

        lax.fori_loop(0, NORM_TM // NORM_ROWS, body, 0)

    @pl.when(pl.program_id(0) < n_ctx_tiles)
    def _():
        run(xc_ref)

    @pl.when(pl.program_id(0) >= n_ctx_tiles)
    def _():
        run(xl_ref)


def _norm_call(xc, xl, mod, norm_g, seq_lat, ctx_row):
    mc, ml = xc.shape[0], xl.shape[0]
    assert mc % NORM_TM == 0 and ml % NORM_TM == 0 and seq_lat % NORM_TM == 0
    ntc = mc // NORM_TM
    tiles_per_seq = seq_lat // NORM_TM
    row = lambda i: jnp.where(i < ntc, ctx_row, (i - ntc) // tiles_per_seq)
    return pl.pallas_call(
        functools.partial(_norm_kernel, n_ctx_tiles=ntc),
        grid=((mc + ml) // NORM_TM,),
        in_specs=[pl.BlockSpec((NORM_TM, D_MODEL), lambda i: (jnp.minimum(i, ntc - 1), 0)),
                  pl.BlockSpec((NORM_TM, D_MODEL), lambda i: (jnp.maximum(i - ntc, 0), 0)),
                  pl.BlockSpec((None, 1, D_MODEL), lambda i: (row(i), 0, 0)),
                  pl.BlockSpec((None, 1, D_MODEL), lambda i: (row(i), 0, 1)),
                  pl.BlockSpec((1, D_MODEL), lambda i: (0, 0))],
        out_specs=pl.BlockSpec((NORM_TM, D_MODEL), lambda i: (i, 0)),
        out_shape=jax.ShapeDtypeStruct((mc + ml, D_MODEL), BF16),
        compiler_params=_cparams(("arbitrary",)),
        name="norm",
    )(xc, xl, mod, mod, norm_g)


IN_TM = 1024
IN_TN = 512
LORA_TILE = N_RKV // IN_TN
Z_LORA_COL = N_RKV
Z_REST_COL = 4 * D_MODEL
Z_COLS = Z_REST_COL + N_REST
REST_TILE_SKIP = Z_REST_COL // IN_TN - (LORA_TILE + 1)


def _inproj_kernel(h_ref, w_ref, wl_ref, mup_ref, mun_ref, o_ref, *, n_ctx_tiles, seq_ctx, seq_lat):
    j = pl.program_id(0)
    i = pl.program_id(1)
    rows = pl.ds(pl.multiple_of(i * IN_TM, IN_TM), IN_TM)

    @pl.when(j <= LORA_TILE)
    def _():
        w = jnp.where(j == LORA_TILE, wl_ref[...], w_ref[...]).astype(BF16)
        z = jnp.dot(h_ref[rows, :], w, preferred_element_type=F32)
        seq_len = jnp.where(i < n_ctx_tiles, seq_ctx, seq_lat)
        pos = lax.broadcasted_iota(jnp.int32, z.shape, 0) & (seq_len - 1)
        zp = jnp.where(pos == 0, 0.0, pltpu.roll(z, 1, 0))
        zn = jnp.where(pos == seq_len - 1, 0.0, pltpu.roll(z, IN_TM - 1, 0))
        o_ref[...] = z + mup_ref[...] * (zp - z) + mun_ref[...] * (zn - z)

    @pl.when(j > LORA_TILE)
    def _():
        o_ref[...] = jnp.dot(h_ref[rows, :], w_ref[...].astype(BF16), preferred_element_type=F32)


def _inproj_call(h, w_in, w_lora, mu_prev, mu_next, m_ctx, seq_ctx, seq_lat):
    m = h.shape[0]
    assert m % IN_TM == 0 and m_ctx % IN_TM == 0 and IN_TM % seq_ctx == 0 and IN_TM % seq_lat == 0
    n_tiles = LORA_TILE + 1 + N_REST // IN_TN

    def w_col(j):
        rest = N_SHIFTED + (j - LORA_TILE - 1) * IN_TN
        col = jnp.where(j < LORA_TILE, j * IN_TN, jnp.where(j == LORA_TILE, (LORA_TILE - 1) * IN_TN, rest))
        return pl.multiple_of(col, 128)

    resident = dict(pipeline_mode=pl.Buffered(1))
    return pl.pallas_call(
        functools.partial(_inproj_kernel, n_ctx_tiles=m_ctx // IN_TM, seq_ctx=seq_ctx, seq_lat=seq_lat),
        grid=(n_tiles, m // IN_TM),
        in_specs=[pl.BlockSpec((m, D_MODEL), lambda j, i: (0, 0), **resident),
                  pl.BlockSpec((pl.Element(D_MODEL), pl.Element(IN_TN)), lambda j, i: (0, w_col(j))),
                  pl.BlockSpec((D_MODEL, IN_TN), lambda j, i: (0, 0), **resident),
                  pl.BlockSpec((1, IN_TN), lambda j, i: (0, jnp.minimum(j, LORA_TILE))),
                  pl.BlockSpec((1, IN_TN), lambda j, i: (0, jnp.minimum(j, LORA_TILE)))],
        out_specs=pl.BlockSpec((IN_TM, IN_TN), lambda j, i: (i, j + jnp.where(j > LORA_TILE, REST_TILE_SKIP, 0))),
        out_shape=jax.ShapeDtypeStruct((m, Z_COLS), F32),
        compiler_params=_cparams(("arbitrary", "arbitrary")),
        name="inproj",
    )(h, w_in, w_lora, mu_prev, mu_next)


def _wkv_kernel(*refs, seq_len, has_s0, emit_state):
    it = iter(refs)
    r_ref, k_ref, v_ref, lora_ref, ga_ref, p8_ref, w0a0_ref, w2a2_ref = (next(it) for _ in range(8))
    s0_refs = (next(it), next(it)) if has_s0 else None
    ya_ref = next(it)
    st_refs = (next(it), next(it)) if emit_state else None
    (atrt_ref, arb_ref, btkt_ref, u0_ref, y0_ref, pc_ref, kk_ref, lw_ref, kd_ref, bb_ref,
     yacc_ref, bonus_ref, s_ref) = (next(it) for _ in range(13))

    n_chunks = seq_len // CHUNK
    n_blocks = seq_len // WKV_ROWS
    p8 = p8_ref[...]
    k_k, k_a, r_k, lnx_g, lnx_b = (p8[i:i + 1] for i in range(5))
    w0a0 = w0a0_ref[...]

    row_c = lax.broadcasted_iota(jnp.int32, (CHUNK, GC), 0)
    col_c = lax.broadcasted_iota(jnp.int32, (CHUNK, GC), 1) & (HEAD - 1)
    strict = (col_c < row_c, col_c > row_c)
    incl = (col_c <= row_c, col_c >= row_c)
    eye_pk = jnp.where(col_c == row_c, 1.0, 0.0)
    same_blk = {}
    m = INV_BASE
    while m <= CHUNK:
        same_blk[m] = (row_c // m) == (col_c // m)
        m *= 2
    rb = lax.broadcasted_iota(jnp.int32, (GC, GC), 0) // HEAD
    cb = lax.broadcasted_iota(jnp.int32, (GC, GC), 1) // HEAD
    bd_mask = rb == cb
    ones_bd16 = jnp.where(bd_mask, 1.0, 0.0).astype(BF16)

    def _bd16(x):
        t = jnp.concatenate([x.astype(F32)] * G_HEADS, axis=0)
        return jnp.where(bd_mask, t, 0.0).astype(BF16)

    def seg_sum(x):
        h1 = x.astype(BF16)
        h2 = (x - h1.astype(F32)).astype(BF16)
        return (jnp.dot(h1, ones_bd16, preferred_element_type=F32)
                + jnp.dot(h2, ones_bd16, preferred_element_type=F32))

    yacc_ref[...] = jnp.zeros_like(yacc_ref)

    def prep(i, carry):
        rows = pl.ds(pl.multiple_of(i * WKV_ROWS, WKV_ROWS), WKV_ROWS)
        r = r_ref[rows, :]
        k = k_ref[rows, :]
        lo = lora_ref[rows, :]
        kk = k * k_k
        kk = kk / jnp.maximum(jnp.sqrt(seg_sum(kk * kk)), 1e-12)
        kk_ref[rows, :] = kk
        kd_sum = None
        for d in range(N_DIR):
            wd = jnp.tanh(lo[:, d * R_PAD:(d + 1) * R_PAD])
            ad = lo[:, (N_DIR + d) * R_PAD:(N_DIR + d + 1) * R_PAD]
            w_lin = w0a0[d:d + 1] + _bdot(wd, w2a2_ref[d])
            lw_ref[d, rows, :] = -EXP_M05 * _sigmoid(w_lin)
            a = _sigmoid(w0a0[N_DIR + d:N_DIR + d + 1] + _bdot(ad, w2a2_ref[N_DIR + d]))
            kd = k * (1.0 + (a - 1.0) * k_a)
            kd_ref[d, rows, :] = kd
            bb_ref[d, rows, :] = kk * a
            kd_sum = kd if kd_sum is None else kd_sum + kd
        bonus_ref[rows, :] = seg_sum(r * kd_sum * r_k) * v_ref[rows, :]
        return carry

    lax.fori_loop(0, n_blocks, prep, 0)

    def running_sum(x, d):
        s = 1
        while s < CHUNK:
            if d == 0:
                x = x + jnp.where(row_c >= s, pltpu.roll(x, s, 0), 0.0)
            else:
                x = x + jnp.where(row_c < CHUNK - s, pltpu.roll(x, CHUNK - s, 0), 0.0)
            s *= 2
        return x

    def phase_a(i, carry):
        jobs = []
        for cc in range(A_CHUNKS):
            c = i * A_CHUNKS + cc
            rows = pl.ds(pl.multiple_of(c * CHUNK, CHUNK), CHUNK)
            r = r_ref[rows, :]
            kk = kk_ref[rows, :]
            for d in range(N_DIR):
                lw = lw_ref[d, rows, :]
                cs = running_sum(lw, d)
                e = jnp.exp(cs)
                e_inv = jnp.exp(-cs)
                rt = r * e
                at = -kk * jnp.exp(cs - lw)
                bt = bb_ref[d, rows, :] * e_inv
                kt = kd_ref[d, rows, :] * e_inv
                atrt_ref[d, c, CHUNK:2 * CHUNK, :] = rt.astype(BF16)
                btkt_ref[d, c, 0:CHUNK, :] = bt.astype(BF16)
                btkt_ref[d, c, CHUNK:2 * CHUNK, :] = kt.astype(BF16)
                edge = e[CHUNK - 1:CHUNK] if d == 0 else e[0:1]
                pc_ref[d, c] = jnp.broadcast_to(edge, (8, GC))
                jobs.append(dict(c=c, d=d, rows=rows, at=at, bt=bt, kt=kt,
                                 ar=jnp.concatenate([at, rt], axis=0).astype(BF16)))
        for j in jobs:
            j["sb"] = _bdot_nt(j["ar"], _bd16(j["bt"]))
        for j in jobs:
            j["sk"] = _bdot_nt(j["ar"], _bd16(j["kt"]))
        for j in jobs:
            d = j["d"]
            j["l_ab"] = jnp.where(strict[d], j["sb"][:CHUNK], 0.0)
            arb_ref[d, j["c"]] = jnp.where(incl[d], j["sb"][CHUNK:], 0.0).astype(BF16)
            j["lak_ark"] = jnp.concatenate([jnp.where(strict[d], j["sk"][:CHUNK], 0.0),
                                            jnp.where(incl[d], j["sk"][CHUNK:], 0.0)], axis=0).astype(BF16)
        for j in jobs:
            j["l_base"] = jnp.where(same_blk[INV_BASE], j["l_ab"], 0.0)
            j["l2"] = _bdot(j["l_base"], _bd16(j["l_base"]))
        for j in jobs:
            tm = eye_pk + j["l_base"]
            j["tm"] = tm + _bdot(tm, _bd16(j["l2"]))
        m = INV_BASE
        while m < CHUNK:
            off = same_blk[2 * m] & jnp.logical_not(same_blk[m])
            for j in jobs:
                j["m1"] = _bdot(j["tm"], _bd16(jnp.where(off, j["l_ab"], 0.0)))
            for j in jobs:
                j["tm"] = j["tm"] + _bdot(j["m1"], _bd16(j["tm"]))
            m *= 2
        for j in jobs:
            atrt_ref[j["d"], j["c"], 0:CHUNK, :] = _bdot(j["tm"], _bd16(j["at"])).astype(BF16)
        for j in jobs:
            j["xv"] = _bdot(j["lak_ark"], _bd16(v_ref[j["rows"], :]))
            y0_ref[j["d"], j["c"]] = j["xv"][CHUNK:]
        for j in jobs:
            u0_ref[j["d"], j["c"]] = _bdot(j["tm"], _bd16(j["xv"][:CHUNK]))
        return carry

    lax.fori_loop(0, n_chunks // A_CHUNKS, phase_a, 0)

    if has_s0:
        tile16 = jnp.where(lax.broadcasted_iota(jnp.int32, (HEAD, GC), 0)
                           == (lax.broadcasted_iota(jnp.int32, (HEAD, GC), 1) & (HEAD - 1)), 1.0, 0.0).astype(BF16)
        for d in range(N_DIR):
            s_ref[d] = jnp.where(bd_mask, _dot_exact_rhs(s0_refs[d][0], tile16), 0.0)
    else:
        s_ref[...] = jnp.zeros_like(s_ref)

    def phase_b(i, carry):
        cs = (i, n_chunks - 1 - i)
        rows = [pl.ds(pl.multiple_of(c * CHUNK, CHUNK), CHUNK) for c in cs]
        s = [s_ref[d] for d in range(N_DIR)]
        us = [_bdot_nt(atrt_ref[d, cs[d]], s[d].astype(BF16)) for d in range(N_DIR)]
        u = [us[d][:CHUNK] + u0_ref[d, cs[d]] for d in range(N_DIR)]
        uv = [jnp.concatenate([u[d].astype(BF16), v_ref[rows[d], :].astype(BF16)], axis=0) for d in range(N_DIR)]
        ds = [_bdot_tn(uv[d], btkt_ref[d, cs[d]]) for d in range(N_DIR)]
        for d in range(N_DIR):
            s_ref[d] = (s[d] + jnp.where(bd_mask, ds[d], 0.0)) * pc_ref[d, cs[d]][0:1]
        for d in range(N_DIR):
            y = us[d][CHUNK:] + _bdot(arb_ref[d, cs[d]], _bd16(u[d])) + y0_ref[d, cs[d]]
            yacc_ref[rows[d], :] += y
        return carry

    lax.fori_loop(0, n_chunks, phase_b, 0)

    if emit_state:
        untile16 = jnp.where((lax.broadcasted_iota(jnp.int32, (GC, HEAD), 0) & (HEAD - 1))
                             == lax.broadcasted_iota(jnp.int32, (GC, HEAD), 1), 1.0, 0.0).astype(BF16)
        for d in range(N_DIR):
            st_refs[d][0] = _dot_exact_rhs(s_ref[d], untile16)

    def phase_c(i, carry):
        rows = pl.ds(pl.multiple_of(i * WKV_ROWS, WKV_ROWS), WKV_ROWS)
        y = yacc_ref[rows, :]
        mu = seg_sum(y) * (1.0 / HEAD)
        yc = y - mu
        var = seg_sum(yc * yc) * (1.0 / HEAD)
        out = yc * lax.rsqrt(var + GN_EPS) * lnx_g + lnx_b + bonus_ref[rows, :]
        ga = ga_ref[rows, :]
        ya_ref[rows, :] = (out * (ga * _sigmoid(ga))).astype(BF16)
        return carry

    lax.fori_loop(0, n_blocks, phase_c, 0)


def _wkv_call(z, row0, m, p8, w0a0, w2a2, seq_len, s0=None, emit_state=False, name="wkv"):
    assert row0 % seq_len == 0 and m % seq_len == 0
    nb = m // seq_len
    b0 = row0 // seq_len
    nc = seq_len // CHUNK
    has_s0 = s0 is not None
    ncb = D_A // GC
    lora_w = 2 * N_DIR * R_PAD
    in_specs = [
        pl.BlockSpec((seq_len, GC), lambda b, g: (b0 + b, g)),
        pl.BlockSpec((seq_len, GC), lambda b, g: (b0 + b, ncb + g)),
        pl.BlockSpec((seq_len, GC), lambda b, g: (b0 + b, 2 * ncb + g)),
        pl.BlockSpec((seq_len, lora_w), lambda b, g: (b0 + b, Z_LORA_COL // lora_w)),
        pl.BlockSpec((seq_len, GC), lambda b, g: (b0 + b, Z_REST_COL // GC + g)),
        pl.BlockSpec((8, GC), lambda b, g: (0, g)),
        pl.BlockSpec((2 * N_DIR, GC), lambda b, g: (0, g)),
        pl.BlockSpec((2 * N_DIR, R_PAD, GC), lambda b, g: (0, 0, g)),
    ]
    args = [z, z, z, z, z, p8, w0a0, w2a2]
    if has_s0:
        in_specs += [pl.BlockSpec((1, GC, HEAD), lambda b, g: (b, g, 0))] * 2
        args += list(s0)
    out_specs = [pl.BlockSpec((seq_len, GC), lambda b, g: (b, g))]
    out_shape = [jax.ShapeDtypeStruct((m, D_A), BF16)]
    if emit_state:
        out_specs += [pl.BlockSpec((1, GC, HEAD), lambda b, g: (b, g, 0))] * 2
        out_shape += [jax.ShapeDtypeStruct((nb, D_A, HEAD), F32)] * 2
    pair16 = pltpu.VMEM((N_DIR, nc, 2 * CHUNK, GC), BF16)
    big16 = pltpu.VMEM((N_DIR, nc, CHUNK, GC), BF16)
    big32 = pltpu.VMEM((N_DIR, nc, CHUNK, GC), F32)
    seq32 = pltpu.VMEM((seq_len, GC), F32)
    dir32 = pltpu.VMEM((N_DIR, seq_len, GC), F32)
    scratch = [pair16, big16, pair16, big32, big32,
               pltpu.VMEM((N_DIR, nc, 8, GC), F32),
               seq32, dir32, dir32, dir32,
               seq32, seq32,
               pltpu.VMEM((N_DIR, GC, GC), F32)]
    return pl.pallas_call(
        functools.partial(_wkv_kernel, seq_len=seq_len, has_s0=has_s0, emit_state=emit_state),
        grid=(nb, N_GROUPS),
        in_specs=in_specs,
        out_specs=out_specs,
        out_shape=out_shape,
        scratch_shapes=scratch,
        compiler_params=_cparams(("arbitrary", "arbitrary")),
        name=name,
    )(*args)


CONV_TM = 256
CONV_TN = 1024


def _conv_kernel(b_ref, c_ref, x_ref, g_ref, w_ref, o_ref, *, row_len):
    p = c_ref[...] * x_ref[...]
    pos = lax.broadcasted_iota(jnp.int32, p.shape, 0) & (row_len - 1)
    pp = jnp.where(pos == 0, 0.0, pltpu.roll(p, 1, 0))
    pn = jnp.where(pos == row_len - 1, 0.0, pltpu.roll(p, CONV_TM - 1, 0))
    w = w_ref[...]
    u = w[0:1] * pp + w[1:2] * p + w[2:3] * pn
    g = g_ref[...]
    o_ref[...] = (b_ref[...] * u * (g * _sigmoid(g))).astype(BF16)


def _conv_call(z, row0, m, conv_w, row_len, name="conv"):
    nb = D_B // CONV_TN
    off = (Z_REST_COL + D_A) // CONV_TN
    i0 = row0 // CONV_TM
    spec = lambda q: pl.BlockSpec((CONV_TM, CONV_TN), lambda i, j: (i0 + i, off + q * nb + j))
    return pl.pallas_call(
        functools.partial(_conv_kernel, row_len=row_len),
        grid=(m // CONV_TM, nb),
        in_specs=[spec(0), spec(1), spec(2), spec(3),
                  pl.BlockSpec((3, CONV_TN), lambda i, j: (0, j))],
        out_specs=pl.BlockSpec((CONV_TM, CONV_TN), lambda i, j: (i, j)),
        out_shape=jax.ShapeDtypeStruct((m, D_B), BF16),
        compiler_params=_cparams(("arbitrary", "arbitrary")),
        name=name,
    )(z, z, z, z, conv_w)


OUT_TM = 256


def _outproj_kernel(ya_ref, yb_ref, ma_ref, mb_ref, x_ref, gate_ref, fg_ref, wa_ref, wb_ref, wo_ref, o_ref):
    y_a = jnp.dot(ya_ref[...], wa_ref[...], preferred_element_type=F32)
    y_b = jnp.dot(yb_ref[...], wb_ref[...], preferred_element_type=F32)
    merged = _sigmoid(ma_ref[...]) * y_a + _sigmoid(mb_ref[...]) * y_b
    out = jnp.dot(merged.astype(BF16), wo_ref[...], preferred_element_type=F32)
    xo = x_ref[...] + gate_ref[...] * out
    ms = jnp.mean(xo * xo, axis=-1, keepdims=True)
    o_ref[...] = xo * lax.rsqrt(ms + RMS_EPS) * fg_ref[...]


def _outproj_call(ya, yb, z, row0, x, mod, final_g, wa16, wb16, wo16, rows_per_mod, mod_row0, mod_row_step,
                  name="outproj"):
    m = x.shape[0]
    mcol = (Z_REST_COL + D_A + 4 * D_B) // D_MODEL
    i0 = row0 // OUT_TM
    row = lambda i: mod_row0 + mod_row_step * (i // rows_per_mod)
    full = lambda: pl.BlockSpec((D_MODEL, D_MODEL), lambda i: (0, 0), pipeline_mode=pl.Buffered(1))
    tile = lambda col: pl.BlockSpec((OUT_TM, D_MODEL), lambda i: (i, col))
    ztile = lambda col: pl.BlockSpec((OUT_TM, D_MODEL), lambda i: (i0 + i, col))
    return pl.pallas_call(
        _outproj_kernel,
        grid=(m // OUT_TM,),
        in_specs=[tile(0), tile(0), ztile(mcol), ztile(mcol + 1), tile(0),
                  pl.BlockSpec((None, 1, D_MODEL), lambda i: (row(i), 0, 2)),
                  pl.BlockSpec((1, D_MODEL), lambda i: (0, 0)),
                  full(), full(), full()],
        out_specs=tile(0),
        out_shape=jax.ShapeDtypeStruct((m, D_MODEL), F32),
        compiler_params=_cparams(("arbitrary",)),
        name=name,
    )(ya, yb, z, z, x, mod, final_g, wa16, wb16, wo16)


def _pad_lora(a, axis):
    shp = a.shape
    a = a.reshape(shp[:axis] + (2 * N_DIR, R_LORA) + shp[axis + 1:])
    pad = [(0, 0)] * a.ndim
    pad[axis + 1] = (0, R_PAD - R_LORA)
    a = jnp.pad(a, pad)
    return a.reshape(shp[:axis] + (2 * N_DIR * R_PAD,) + shp[axis + 1:])


def _stream(x, z, row0, seq_len, mod, mod_row0, mod_row_step, conv_rows, lw, s0, emit_state, tag):
    (p8, w0a0, w2a2, conv_w, wa16, wb16, wo16, final_g) = lw
    m = x.shape[0]
    wkv_out = _wkv_call(z, row0, m, p8, w0a0, w2a2, seq_len, s0=s0, emit_state=emit_state, name="wkv_" + tag)
    ya = wkv_out[0]
    yb = _conv_call(z, row0, m, conv_w, seq_len // conv_rows, name="conv_" + tag)
    y = _outproj_call(ya, yb, z, row0, x, mod, final_g, wa16, wb16, wo16, seq_len // OUT_TM,
                      mod_row0, mod_row_step, name="outproj_" + tag)
    return y, wkv_out[1:]


def kernel(x_prompt, x_sample, state_wkv_fwd, state_wkv_bwd, c, c_ctx, ada_w, ada_b, norm_g, w_in, mu_prev,
           mu_next, w0, w2, a0, a2, k_k, k_a, r_k, lnx_g, lnx_b, conv_w, w_out_a, w_out_b, w_o, final_g):
    depth = norm_g.shape[0]
    assert depth == 1
    l = 0
    nbp, seq_p, _ = x_prompt.shape
    nbs, seq_s, _ = x_sample.shape
    lat_rows = seq_s // GRID_W
    xc = x_prompt.reshape(nbp * seq_p, D_MODEL)
    xl = x_sample.reshape(nbs * seq_s, D_MODEL)

    cvec = jnp.concatenate([c, c_ctx[None, :], jnp.zeros((8 - nbs - 1, D_MODEL), F32)], axis=0)
    mod = _ada_call(cvec, ada_w[l], ada_b[l][None, :]).reshape(8, 1, 3 * D_MODEL)

    w_lora = _pad_lora(w_in[l][:, N_RKV:N_SHIFTED], 1)
    mu_p = jnp.concatenate([mu_prev[l][None, :N_RKV], _pad_lora(mu_prev[l][None, N_RKV:], 1)], axis=1)
    mu_n = jnp.concatenate([mu_next[l][None, :N_RKV], _pad_lora(mu_next[l][None, N_RKV:], 1)], axis=1)
    zrow = jnp.zeros((D_A,), F32)
    p8 = jnp.stack([k_k[l], k_a[l], r_k[l].reshape(D_A), lnx_g[l], lnx_b[l], zrow, zrow, zrow], axis=0)
    w0a0 = jnp.concatenate([w0[l], a0[l]], axis=0)
    w2a2 = jnp.pad(jnp.concatenate([w2[l], a2[l]], axis=0), ((0, 0), (0, R_PAD - R_LORA), (0, 0)))
    lw = (p8, w0a0, w2a2, conv_w[l],
          w_out_a[l].astype(BF16), w_out_b[l].astype(BF16), w_o[l].astype(BF16), final_g[None, :])

    h = _norm_call(xc, xl, mod, norm_g[l][None, :], seq_s, nbs)
    z = _inproj_call(h, w_in[l], w_lora, mu_p, mu_n, xc.shape[0], seq_p, seq_s)

    y_prompt, st = _stream(xc, z, 0, seq_p, mod, nbs, 0, 1, lw, None, True, "ctx")
    s0 = (state_wkv_fwd[:, l].reshape(nbs, D_A, HEAD), state_wkv_bwd[:, l].reshape(nbs, D_A, HEAD))
    y_sample, _ = _stream(xl, z, xc.shape[0], seq_s, mod, 0, 1, lat_rows, lw, s0, False, "lat")
    new_f = st[0].reshape(nbp, 1, N_HEADS, HEAD, HEAD)
    new_b = st[1].reshape(nbp, 1, N_HEADS, HEAD, HEAD)
    return (y_prompt.reshape(nbp, seq_p, D_MODEL), y_sample.reshape(nbs, seq_s, D_MODEL), new_f, new_b)
```

```python
import functools

import jax
import jax.numpy as jnp
from jax import lax
from jax.experimental import pallas as pl
from jax.experimental.pallas import tpu as pltpu

F32 = jnp.float32
BF16 = jnp.bfloat16

D_MODEL = 2048
D_A = 2048
D_B = 2048
HEAD = 64
N_HEADS = D_A // HEAD
R_LORA = 96
R_PAD = 128
N_DIR = 2
GRID_W = 64
RMS_EPS = 1e-6
GN_EPS = HEAD * 1e-5
N_RKV = 3 * D_A
N_LORA = 2 * N_DIR * R_LORA
N_SHIFTED = N_RKV + N_LORA
N_REST = D_A + 4 * D_B + 2 * D_MODEL

CHUNK = 64
INV_BASE = 4
G_HEADS = 4
GC = G_HEADS * HEAD
N_GROUPS = N_HEADS // G_HEADS
WKV_ROWS = 256

V7X_VMEM_LIMIT = 60 * 1024 * 1024


def _cparams(sem):
    return pltpu.CompilerParams(dimension_semantics=sem, vmem_limit_bytes=V7X_VMEM_LIMIT)


def _bdot(a, b):
    return jnp.dot(a.astype(BF16), b.astype(BF16), preferred_element_type=F32)


def _bdot_nt(a, b):
    return lax.dot_general(a.astype(BF16), b.astype(BF16), (((1,), (1,)), ((), ())),
                           preferred_element_type=F32)


def _bdot_tn(a, b):
    return lax.dot_general(a.astype(BF16), b.astype(BF16), (((0,), (0,)), ((), ())),
                           preferred_element_type=F32)


def _split3(x):
    h1 = x.astype(BF16)
    r1 = x - h1.astype(F32)
    h2 = r1.astype(BF16)
    h3 = (r1 - h2.astype(F32)).astype(BF16)
    return h1, h2, h3


def _dot_exact_rhs(x, m16):
    h1, h2, h3 = _split3(x)
    d = lambda a: jnp.dot(a, m16, preferred_element_type=F32)
    return d(h1) + d(h2) + d(h3)


def _sigmoid(x):
    return 1.0 / (1.0 + jnp.exp(-x))


EXP_M05 = 0.6065306597126334


def _ada_kernel(c_ref, w_ref, b_ref, o_ref):
    c = c_ref[...]
    s = c * _sigmoid(c)
    o_ref[...] = jnp.dot(s, w_ref[...], precision=lax.Precision.HIGHEST,
                         preferred_element_type=F32) + b_ref[...]


def _ada_call(cvec, ada_w, ada_b):
    n = ada_w.shape[1]
    tn = 1024
    return pl.pallas_call(
        _ada_kernel,
        grid=(n // tn,),
        in_specs=[pl.BlockSpec((8, D_MODEL), lambda j: (0, 0)),
                  pl.BlockSpec((D_MODEL, tn), lambda j: (0, j)),
                  pl.BlockSpec((1, tn), lambda j: (0, j))],
        out_specs=pl.BlockSpec((8, tn), lambda j: (0, j)),
        out_shape=jax.ShapeDtypeStruct((8, n), F32),
        compiler_params=_cparams(("arbitrary",)),
        name="ada",
    )(cvec, ada_w, ada_b)


NORM_TM = 512
NORM_ROWS = 128


def _norm_kernel(xc_ref, xl_ref, sh_ref, sc_ref, g_ref, o_ref, *, n_ctx_tiles):
    g = g_ref[...]
    sc = 1.0 + sc_ref[...]
    sh = sh_ref[...]

    def run(x_ref):
        def body(i, carry):
            rows = pl.ds(pl.multiple_of(i * NORM_ROWS, NORM_ROWS), NORM_ROWS)
            x = x_ref[rows, :]
            ms = jnp.mean(x * x, axis=-1, keepdims=True)
            xn = x * lax.rsqrt(ms + RMS_EPS) * g
            o_ref[rows, :] = (xn * sc + sh).astype(BF16)
            return carry

        lax.fori_loop(0, NORM_TM // NORM_ROWS, body, 0)

    @pl.when(pl.program_id(0) < n_ctx_tiles)
    def _():
        run(xc_ref)

    @pl.when(pl.program_id(0) >= n_ctx_tiles)
    def _():
        run(xl_ref)


def _norm_call(xc, xl, mod, norm_g, seq_lat, ctx_row):
    mc, ml = xc.shape[0], xl.shape[0]
    assert mc % NORM_TM == 0 and ml % NORM_TM == 0 and seq_lat % NORM_TM == 0
    ntc = mc // NORM_TM
    tiles_per_seq = seq_lat // NORM_TM
    row = lambda i: jnp.where(i < ntc, ctx_row, (i - ntc) // tiles_per_seq)
    return pl.pallas_call(
        functools.partial(_norm_kernel, n_ctx_tiles=ntc),
        grid=((mc + ml) // NORM_TM,),
        in_specs=[pl.BlockSpec((NORM_TM, D_MODEL), lambda i: (jnp.minimum(i, ntc - 1), 0)),
                  pl.BlockSpec((NORM_TM, D_MODEL), lambda i: (jnp.maximum(i - ntc, 0), 0)),
                  pl.BlockSpec((None, 1, D_MODEL), lambda i: (row(i), 0, 0)),
                  pl.BlockSpec((None, 1, D_MODEL), lambda i: (row(i), 0, 1)),
                  pl.BlockSpec((1, D_MODEL), lambda i: (0, 0))],
        out_specs=pl.BlockSpec((NORM_TM, D_MODEL), lambda i: (i, 0)),
        out_shape=jax.ShapeDtypeStruct((mc + ml, D_MODEL), BF16),
        compiler_params=_cparams(("arbitrary",)),
        name="norm",
    )(xc, xl, mod, mod, norm_g)


IN_TM = 1024
IN_TN = 512
LORA_TILE = N_RKV // IN_TN
Z_LORA_COL = N_RKV
Z_REST_COL = 4 * D_MODEL
Z_COLS = Z_REST_COL + N_REST
REST_TILE_SKIP = Z_REST_COL // IN_TN - (LORA_TILE + 1)


def _inproj_kernel(h_ref, w_ref, wl_ref, mup_ref, mun_ref, o_ref, *, n_ctx_tiles, seq_ctx, seq_lat):
    j = pl.program_id(0)
    i = pl.program_id(1)
    rows = pl.ds(pl.multiple_of(i * IN_TM, IN_TM), IN_TM)

    @pl.when(j <= LORA_TILE)
    def _():
        w = jnp.where(j == LORA_TILE, wl_ref[...], w_ref[...]).astype(BF16)
        z = jnp.dot(h_ref[rows, :], w, preferred_element_type=F32)
        seq_len = jnp.where(i < n_ctx_tiles, seq_ctx, seq_lat)
        pos = lax.broadcasted_iota(jnp.int32, z.shape, 0) & (seq_len - 1)
        zp = jnp.where(pos == 0, 0.0, pltpu.roll(z, 1, 0))
        zn = jnp.where(pos == seq_len - 1, 0.0, pltpu.roll(z, IN_TM - 1, 0))
        o_ref[...] = z + mup_ref[...] * (zp - z) + mun_ref[...] * (zn - z)

    @pl.when(j > LORA_TILE)
    def _():
        o_ref[...] = jnp.dot(h_ref[rows, :], w_ref[...].astype(BF16), preferred_element_type=F32)


def _inproj_call(h, w_in, w_lora, mu_prev, mu_next, m_ctx, seq_ctx, seq_lat):
    m = h.shape[0]
    assert m % IN_TM == 0 and m_ctx % IN_TM == 0 and IN_TM % seq_ctx == 0 and IN_TM % seq_lat == 0
    n_tiles = LORA_TILE + 1 + N_REST // IN_TN

    def w_col(j):
        rest = N_SHIFTED + (j - LORA_TILE - 1) * IN_TN
        col = jnp.where(j < LORA_TILE, j * IN_TN, jnp.where(j == LORA_TILE, (LORA_TILE - 1) * IN_TN, rest))
        return pl.multiple_of(col, 128)

    resident = dict(pipeline_mode=pl.Buffered(1))
    return pl.pallas_call(
        functools.partial(_inproj_kernel, n_ctx_tiles=m_ctx // IN_TM, seq_ctx=seq_ctx, seq_lat=seq_lat),
        grid=(n_tiles, m // IN_TM),
        in_specs=[pl.BlockSpec((m, D_MODEL), lambda j, i: (0, 0), **resident),
                  pl.BlockSpec((pl.Element(D_MODEL), pl.Element(IN_TN)), lambda j, i: (0, w_col(j))),
                  pl.BlockSpec((D_MODEL, IN_TN), lambda j, i: (0, 0), **resident),
                  pl.BlockSpec((1, IN_TN), lambda j, i: (0, jnp.minimum(j, LORA_TILE))),
                  pl.BlockSpec((1, IN_TN), lambda j, i: (0, jnp.minimum(j, LORA_TILE)))],
        out_specs=pl.BlockSpec((IN_TM, IN_TN), lambda j, i: (i, j + jnp.where(j > LORA_TILE, REST_TILE_SKIP, 0))),
        out_shape=jax.ShapeDtypeStruct((m, Z_COLS), F32),
        compiler_params=_cparams(("arbitrary", "arbitrary")),
        name="inproj",
    )(h, w_in, w_lora, mu_prev, mu_next)


WKV_BLK = 1024
N_STAGE = 4
STAGE_CHUNKS = WKV_BLK // CHUNK // N_STAGE


def _lockstep(tasks):
    tasks = list(tasks)
    while tasks:
        alive = []
        for t in tasks:
            try:
                next(t)
                alive.append(t)
            except StopIteration:
                pass
        tasks = alive


def _wkv_kernel(*refs, seq_len, has_s0, emit_state):
    it = iter(refs)
    r_ref, k_ref, v_ref, lora_ref, ga_ref, p8_ref, w0a0_ref, w2a2_ref = (next(it) for _ in range(8))
    s0_refs = (next(it), next(it)) if has_s0 else None
    ya_ref = next(it)
    st_refs = (next(it), next(it)) if emit_state else None
    (atrt_ref, arb_ref, btkt_ref, u0_ref, y0_ref, pc_ref, kk_ref, lw_ref, kd_ref, bb_ref,
     yacc_ref, bonus_ref, s_ref) = (next(it) for _ in range(13))

    n_seq = WKV_BLK // seq_len
    n_chunks = WKV_BLK // CHUNK
    p8 = p8_ref[...]
    k_k, k_a, r_k, lnx_g, lnx_b = (p8[i:i + 1] for i in range(5))
    w0a0 = w0a0_ref[...]

    row_c = lax.broadcasted_iota(jnp.int32, (CHUNK, GC), 0)
    col_c = lax.broadcasted_iota(jnp.int32, (CHUNK, GC), 1) & (HEAD - 1)
    strict = (col_c < row_c, col_c > row_c)
    incl = (col_c <= row_c, col_c >= row_c)
    eye_pk = jnp.where(col_c == row_c, 1.0, 0.0)
    same_blk = {}
    m = INV_BASE
    while m <= CHUNK:
        same_blk[m] = (row_c // m) == (col_c // m)
        m *= 2
    rb = lax.broadcasted_iota(jnp.int32, (GC, GC), 0) // HEAD
    cb = lax.broadcasted_iota(jnp.int32, (GC, GC), 1) // HEAD
    bd_mask = rb == cb
    ones_bd16 = jnp.where(bd_mask, 1.0, 0.0).astype(BF16)

    def bd16(x):
        t = jnp.concatenate([x.astype(F32)] * G_HEADS, axis=0)
        return jnp.where(bd_mask, t, 0.0).astype(BF16)

    def split2(x):
        h1 = x.astype(BF16)
        return h1, (x - h1.astype(F32)).astype(BF16)

    def ones_dot(x16):
        return jnp.dot(x16, ones_bd16, preferred_element_type=F32)

    def running_sum(x, d):
        s = 1
        while s < CHUNK:
            if d == 0:
                x = x + jnp.where(row_c >= s, pltpu.roll(x, s, 0), 0.0)
            else:
                x = x + jnp.where(row_c < CHUNK - s, pltpu.roll(x, CHUNK - s, 0), 0.0)
            s *= 2
        return x

    def prep(j):
        rows = pl.ds(j * WKV_ROWS, WKV_ROWS)
        r = r_ref[rows, :]
        k = k_ref[rows, :]
        lo = lora_ref[rows, :]
        kk = k * k_k
        q1, q2 = split2(kk * kk)
        ssq = ones_dot(q1)
        yield
        ssq = ssq + ones_dot(q2)
        yield
        kk = kk / jnp.maximum(jnp.sqrt(ssq), 1e-12)
        kk_ref[rows, :] = kk
        kd_sum = None
        for d in range(N_DIR):
            wd = jnp.tanh(lo[:, d * R_PAD:(d + 1) * R_PAD])
            ad = lo[:, (N_DIR + d) * R_PAD:(N_DIR + d + 1) * R_PAD]
            w_lin = w0a0[d:d + 1] + _bdot(wd, w2a2_ref[d])
            yield
            lw_ref[d, rows, :] = -EXP_M05 * _sigmoid(w_lin)
            a = _sigmoid(w0a0[N_DIR + d:N_DIR + d + 1] + _bdot(ad, w2a2_ref[N_DIR + d]))
            yield
            kd = k * (1.0 + (a - 1.0) * k_a)
            kd_ref[d, rows, :] = kd
            bb_ref[d, rows, :] = kk * a
            kd_sum = kd if kd_sum is None else kd_sum + kd
        b1, b2 = split2(r * kd_sum * r_k)
        bs = ones_dot(b1)
        yield
        bs = bs + ones_dot(b2)
        yield
        bonus_ref[rows, :] = bs * v_ref[rows, :]

    def a_job(c, d):
        rows = pl.ds(c * CHUNK, CHUNK)
        r = r_ref[rows, :]
        kk = kk_ref[rows, :]
        lw = lw_ref[d, rows, :]
        cs = running_sum(lw, d)
        e = jnp.exp(cs)
        e_inv = jnp.exp(-cs)
        rt = r * e
        at = -kk * jnp.exp(cs - lw)
        bt = bb_ref[d, rows, :] * e_inv
        kt = kd_ref[d, rows, :] * e_inv
        atrt_ref[d, c, CHUNK:2 * CHUNK, :] = rt.astype(BF16)
        btkt_ref[d, c, 0:CHUNK, :] = bt.astype(BF16)
        btkt_ref[d, c, CHUNK:2 * CHUNK, :] = kt.astype(BF16)
        edge = e[CHUNK - 1:CHUNK] if d == 0 else e[0:1]
        pc_ref[d, c] = jnp.broadcast_to(edge, (8, GC))
        ar = jnp.concatenate([at, rt], axis=0).astype(BF16)
        sb = _bdot_nt(ar, bd16(bt))
        yield
        sk = _bdot_nt(ar, bd16(kt))
        yield
        l_ab = jnp.where(strict[d], sb[:CHUNK], 0.0)
        arb_ref[d, c] = jnp.where(incl[d], sb[CHUNK:], 0.0).astype(BF16)
        lak_ark = jnp.concatenate([jnp.where(strict[d], sk[:CHUNK], 0.0),
                                   jnp.where(incl[d], sk[CHUNK:], 0.0)], axis=0).astype(BF16)
        l_base = jnp.where(same_blk[INV_BASE], l_ab, 0.0)
        l2 = _bdot(l_base, bd16(l_base))
        yield
        tm = eye_pk + l_base
        tm = tm + _bdot(tm, bd16(l2))
        yield
        m = INV_BASE
        while m < CHUNK:
            off = same_blk[2 * m] & jnp.logical_not(same_blk[m])
            m1 = _bdot(tm, bd16(jnp.where(off, l_ab, 0.0)))
            yield
            tm = tm + _bdot(m1, bd16(tm))
            yield
            m *= 2
        atrt_ref[d, c, 0:CHUNK, :] = _bdot(tm, bd16(at)).astype(BF16)
        yield
        xv = _bdot(lak_ark, bd16(v_ref[rows, :]))
        yield
        y0_ref[d, c] = xv[CHUNK:]
        u0_ref[d, c] = _bdot(tm, bd16(xv[:CHUNK]))
        yield

    def b_chain(d, chunks, init, out_seq):
        if init == "zero":
            s_ref[d] = jnp.zeros((GC, GC), F32)
        elif init == "given":
            tile16 = jnp.where(lax.broadcasted_iota(jnp.int32, (HEAD, GC), 0)
                               == (lax.broadcasted_iota(jnp.int32, (HEAD, GC), 1) & (HEAD - 1)),
                               1.0, 0.0).astype(BF16)
            s_ref[d] = jnp.where(bd_mask, _dot_exact_rhs(s0_refs[d][0], tile16), 0.0)
            yield
        for c in chunks:
            rows = pl.ds(c * CHUNK, CHUNK)
            s = s_ref[d]
            us = _bdot_nt(atrt_ref[d, c], s.astype(BF16))
            yield
            u = us[:CHUNK] + u0_ref[d, c]
            uv = jnp.concatenate([u.astype(BF16), v_ref[rows, :].astype(BF16)], axis=0)
            ds = _bdot_tn(uv, btkt_ref[d, c])
            yield
            s_ref[d] = (s + jnp.where(bd_mask, ds, 0.0)) * pc_ref[d, c][0:1]
            y = us[CHUNK:] + _bdot(arb_ref[d, c], bd16(u)) + y0_ref[d, c]
            yield
            yacc_ref[rows, :] += y
        if out_seq is not None:
            untile16 = jnp.where((lax.broadcasted_iota(jnp.int32, (GC, HEAD), 0) & (HEAD - 1))
                                 == lax.broadcasted_iota(jnp.int32, (GC, HEAD), 1), 1.0, 0.0).astype(BF16)
            st_refs[d][out_seq] = _dot_exact_rhs(s_ref[d], untile16)
            yield

    def post(j):
        rows = pl.ds(j * WKV_ROWS, WKV_ROWS)
        y = yacc_ref[rows, :]
        y1, y2 = split2(y)
        mu = ones_dot(y1)
        yield
        mu = (mu + ones_dot(y2)) * (1.0 / HEAD)
        yield
        yc = y - mu
        v1, v2 = split2(yc * yc)
        var = ones_dot(v1)
        yield
        var = (var + ones_dot(v2)) * (1.0 / HEAD)
        yield
        out = yc * lax.rsqrt(var + GN_EPS) * lnx_g + lnx_b + bonus_ref[rows, :]
        ga = ga_ref[rows, :]
        ya_ref[rows, :] = (out * (ga * _sigmoid(ga))).astype(BF16)

    yacc_ref[...] = jnp.zeros_like(yacc_ref)
    sc = STAGE_CHUNKS
    if n_seq == 1:
        fwd = [[i * sc + q for q in range(sc)] for i in range(N_STAGE)]
        bwd = [[n_chunks - 1 - i * sc - q for q in range(sc)] for i in range(N_STAGE)]
        a_jobs = [[a_job(c, 0) for c in fwd[i]] + [a_job(c, 1) for c in bwd[i]] for i in range(N_STAGE)]
        chains = [[b_chain(0, fwd[i], "given" if (has_s0 and i == 0) else ("zero" if i == 0 else None), None),
                   b_chain(1, bwd[i], "given" if (has_s0 and i == 0) else ("zero" if i == 0 else None), None)]
                  for i in range(N_STAGE)]
        stages = [
            [prep(0), prep(N_STAGE - 1)],
            a_jobs[0] + [prep(j) for j in range(1, N_STAGE - 1)],
        ]
        for i in range(1, N_STAGE):
            stages.append(a_jobs[i] + chains[i - 1])
        stages.append(chains[N_STAGE - 1] + [post(j) for j in range(1, N_STAGE - 1)])
        stages.append([post(0), post(N_STAGE - 1)])
    else:
        assert n_seq == N_STAGE and seq_len == sc * CHUNK
        fwd = [[i * sc + q for q in range(sc)] for i in range(N_STAGE)]
        bwd = [list(reversed(f)) for f in fwd]
        a_jobs = [[a_job(c, 0) for c in fwd[i]] + [a_job(c, 1) for c in bwd[i]] for i in range(N_STAGE)]
        out_seq = (lambda i: i) if emit_state else (lambda i: None)
        chains = [[b_chain(0, fwd[i], "zero", out_seq(i)), b_chain(1, bwd[i], "zero", out_seq(i))]
                  for i in range(N_STAGE)]
        stages = [[prep(0)], a_jobs[0] + [prep(1)]]
        for i in range(1, N_STAGE):
            stage = a_jobs[i] + chains[i - 1]
            if i + 1 < N_STAGE:
                stage.append(prep(i + 1))
            if i >= 2:
                stage.append(post(i - 2))
            stages.append(stage)
        stages.append(chains[N_STAGE - 1] + [post(N_STAGE - 2)])
        stages.append([post(N_STAGE - 1)])
    for stage in stages:
        _lockstep(stage)


def _wkv_call(z, row0, m, p8, w0a0, w2a2, seq_len, s0=None, emit_state=False, name="wkv"):
    assert row0 % WKV_BLK == 0 and m % WKV_BLK == 0 and WKV_BLK % seq_len == 0
    n_seq = WKV_BLK // seq_len
    nb = m // WKV_BLK
    b0 = row0 // WKV_BLK
    nc = WKV_BLK // CHUNK
    has_s0 = s0 is not None
    assert not has_s0 or n_seq == 1
    ncb = D_A // GC
    lora_w = 2 * N_DIR * R_PAD
    in_specs = [
        pl.BlockSpec((WKV_BLK, GC), lambda b, g: (b0 + b, g)),
        pl.BlockSpec((WKV_BLK, GC), lambda b, g: (b0 + b, ncb + g)),
        pl.BlockSpec((WKV_BLK, GC), lambda b, g: (b0 + b, 2 * ncb + g)),
        pl.BlockSpec((WKV_BLK, lora_w), lambda b, g: (b0 + b, Z_LORA_COL // lora_w)),
        pl.BlockSpec((WKV_BLK, GC), lambda b, g: (b0 + b, Z_REST_COL // GC + g)),
        pl.BlockSpec((8, GC), lambda b, g: (0, g)),
        pl.BlockSpec((2 * N_DIR, GC), lambda b, g: (0, g)),
        pl.BlockSpec((2 * N_DIR, R_PAD, GC), lambda b, g: (0, 0, g)),
    ]
    args = [z, z, z, z, z, p8, w0a0, w2a2]
    if has_s0:
        in_specs += [pl.BlockSpec((1, GC, HEAD), lambda b, g: (b, g, 0))] * 2
        args += list(s0)
    out_specs = [pl.BlockSpec((WKV_BLK, GC), lambda b, g: (b, g))]
    out_shape = [jax.ShapeDtypeStruct((m, D_A), BF16)]
    if emit_state:
        out_specs += [pl.BlockSpec((n_seq, GC, HEAD), lambda b, g: (b, g, 0))] * 2
        out_shape += [jax.ShapeDtypeStruct((nb * n_seq, D_A, HEAD), F32)] * 2
    pair16 = pltpu.VMEM((N_DIR, nc, 2 * CHUNK, GC), BF16)
    big16 = pltpu.VMEM((N_DIR, nc, CHUNK, GC), BF16)
    big32 = pltpu.VMEM((N_DIR, nc, CHUNK, GC), F32)
    seq32 = pltpu.VMEM((WKV_BLK, GC), F32)
    dir32 = pltpu.VMEM((N_DIR, WKV_BLK, GC), F32)
    scratch = [pair16, big16, pair16, big32, big32,
               pltpu.VMEM((N_DIR, nc, 8, GC), F32),
               seq32, dir32, dir32, dir32,
               seq32, seq32,
               pltpu.VMEM((N_DIR, GC, GC), F32)]
    return pl.pallas_call(
        functools.partial(_wkv_kernel, seq_len=seq_len, has_s0=has_s0, emit_state=emit_state),
        grid=(nb, N_GROUPS),
        in_specs=in_specs,
        out_specs=out_specs,
        out_shape=out_shape,
        scratch_shapes=scratch,
        compiler_params=_cparams(("arbitrary", "arbitrary")),
        name=name,
    )(*args)


CONV_TM = 256
CONV_TN = 1024


def _conv_kernel(b_ref, c_ref, x_ref, g_ref, w_ref, o_ref, *, row_len):
    p = c_ref[...] * x_ref[...]
    pos = lax.broadcasted_iota(jnp.int32, p.shape, 0) & (row_len - 1)
    pp = jnp.where(pos == 0, 0.0, pltpu.roll(p, 1, 0))
    pn = jnp.where(pos == row_len - 1, 0.0, pltpu.roll(p, CONV_TM - 1, 0))
    w = w_ref[...]
    u = w[0:1] * pp + w[1:2] * p + w[2:3] * pn
    g = g_ref[...]
    o_ref[...] = (b_ref[...] * u * (g * _sigmoid(g))).astype(BF16)


def _conv_call(z, row0, m, conv_w, row_len, name="conv"):
    nb = D_B // CONV_TN
    off = (Z_REST_COL + D_A) // CONV_TN
    i0 = row0 // CONV_TM
    spec = lambda q: pl.BlockSpec((CONV_TM, CONV_TN), lambda i, j: (i0 + i, off + q * nb + j))
    return pl.pallas_call(
        functools.partial(_conv_kernel, row_len=row_len),
        grid=(m // CONV_TM, nb),
        in_specs=[spec(0), spec(1), spec(2), spec(3),
                  pl.BlockSpec((3, CONV_TN), lambda i, j: (0, j))],
        out_specs=pl.BlockSpec((CONV_TM, CONV_TN), lambda i, j: (i, j)),
        out_shape=jax.ShapeDtypeStruct((m, D_B), BF16),
        compiler_params=_cparams(("arbitrary", "arbitrary")),
        name=name,
    )(z, z, z, z, conv_w)


OUT_TM = 256


def _outproj_kernel(ya_ref, yb_ref, ma_ref, mb_ref, x_ref, gate_ref, fg_ref, wa_ref, wb_ref, wo_ref, o_ref):
    y_a = jnp.dot(ya_ref[...], wa_ref[...], preferred_element_type=F32)
    y_b = jnp.dot(yb_ref[...], wb_ref[...], preferred_element_type=F32)
    merged = _sigmoid(ma_ref[...]) * y_a + _sigmoid(mb_ref[...]) * y_b
    out = jnp.dot(merged.astype(BF16), wo_ref[...], preferred_element_type=F32)
    xo = x_ref[...] + gate_ref[...] * out
    ms = jnp.mean(xo * xo, axis=-1, keepdims=True)
    o_ref[...] = xo * lax.rsqrt(ms + RMS_EPS) * fg_ref[...]


def _outproj_call(ya, yb, z, row0, x, mod, final_g, wa16, wb16, wo16, rows_per_mod, mod_row0, mod_row_step,
                  name="outproj"):
    m = x.shape[0]
    mcol = (Z_REST_COL + D_A + 4 * D_B) // D_MODEL
    i0 = row0 // OUT_TM
    row = lambda i: mod_row0 + mod_row_step * (i // rows_per_mod)
    full = lambda: pl.BlockSpec((D_MODEL, D_MODEL), lambda i: (0, 0), pipeline_mode=pl.Buffered(1))
    tile = lambda col: pl.BlockSpec((OUT_TM, D_MODEL), lambda i: (i, col))
    ztile = lambda col: pl.BlockSpec((OUT_TM, D_MODEL), lambda i: (i0 + i, col))
    return pl.pallas_call(
        _outproj_kernel,
        grid=(m // OUT_TM,),
        in_specs=[tile(0), tile(0), ztile(mcol), ztile(mcol + 1), tile(0),
                  pl.BlockSpec((None, 1, D_MODEL), lambda i: (row(i), 0, 2)),
                  pl.BlockSpec((1, D_MODEL), lambda i: (0, 0)),
                  full(), full(), full()],
        out_specs=tile(0),
        out_shape=jax.ShapeDtypeStruct((m, D_MODEL), F32),
        compiler_params=_cparams(("arbitrary",)),
        name=name,
    )(ya, yb, z, z, x, mod, final_g, wa16, wb16, wo16)


def _pad_lora(a, axis):
    shp = a.shape
    a = a.reshape(shp[:axis] + (2 * N_DIR, R_LORA) + shp[axis + 1:])
    pad = [(0, 0)] * a.ndim
    pad[axis + 1] = (0, R_PAD - R_LORA)
    a = jnp.pad(a, pad)
    return a.reshape(shp[:axis] + (2 * N_DIR * R_PAD,) + shp[axis + 1:])


def _stream(x, z, row0, seq_len, mod, mod_row0, mod_row_step, conv_rows, lw, s0, emit_state, tag):
    (p8, w0a0, w2a2, conv_w, wa16, wb16, wo16, final_g) = lw
    m = x.shape[0]
    wkv_out = _wkv_call(z, row0, m, p8, w0a0, w2a2, seq_len, s0=s0, emit_state=emit_state, name="wkv_" + tag)
    ya = wkv_out[0]
    yb = _conv_call(z, row0, m, conv_w, seq_len // conv_rows, name="conv_" + tag)
    y = _outproj_call(ya, yb, z, row0, x, mod, final_g, wa16, wb16, wo16, seq_len // OUT_TM,
                      mod_row0, mod_row_step, name="outproj_" + tag)
    return y, wkv_out[1:]


def kernel(x_prompt, x_sample, state_wkv_fwd, state_wkv_bwd, c, c_ctx, ada_w, ada_b, norm_g, w_in, mu_prev,
           mu_next, w0, w2, a0, a2, k_k, k_a, r_k, lnx_g, lnx_b, conv_w, w_out_a, w_out_b, w_o, final_g):
    depth = norm_g.shape[0]
    assert depth == 1
    l = 0
    nbp, seq_p, _ = x_prompt.shape
    nbs, seq_s, _ = x_sample.shape
    lat_rows = seq_s // GRID_W
    xc = x_prompt.reshape(nbp * seq_p, D_MODEL)
    xl = x_sample.reshape(nbs * seq_s, D_MODEL)

    cvec = jnp.concatenate([c, c_ctx[None, :], jnp.zeros((8 - nbs - 1, D_MODEL), F32)], axis=0)
    mod = _ada_call(cvec, ada_w[l], ada_b[l][None, :]).reshape(8, 1, 3 * D_MODEL)

    w_lora = _pad_lora(w_in[l][:, N_RKV:N_SHIFTED], 1)
    mu_p = jnp.concatenate([mu_prev[l][None, :N_RKV], _pad_lora(mu_prev[l][None, N_RKV:], 1)], axis=1)
    mu_n = jnp.concatenate([mu_next[l][None, :N_RKV], _pad_lora(mu_next[l][None, N_RKV:], 1)], axis=1)
    zrow = jnp.zeros((D_A,), F32)
    p8 = jnp.stack([k_k[l], k_a[l], r_k[l].reshape(D_A), lnx_g[l], lnx_b[l], zrow, zrow, zrow], axis=0)
    w0a0 = jnp.concatenate([w0[l], a0[l]], axis=0)
    w2a2 = jnp.pad(jnp.concatenate([w2[l], a2[l]], axis=0), ((0, 0), (0, R_PAD - R_LORA), (0, 0)))
    lw = (p8, w0a0, w2a2, conv_w[l],
          w_out_a[l].astype(BF16), w_out_b[l].astype(BF16), w_o[l].astype(BF16), final_g[None, :])

    h = _norm_call(xc, xl, mod, norm_g[l][None, :], seq_s, nbs)
    z = _inproj_call(h, w_in[l], w_lora, mu_p, mu_n, xc.shape[0], seq_p, seq_s)

    y_prompt, st = _stream(xc, z, 0, seq_p, mod, nbs, 0, 1, lw, None, True, "ctx")
    s0 = (state_wkv_fwd[:, l].reshape(nbs, D_A, HEAD), state_wkv_bwd[:, l].reshape(nbs, D_A, HEAD))
    y_sample, _ = _stream(xl, z, xc.shape[0], seq_s, mod, 0, 1, lat_rows, lw, s0, False, "lat")
    new_f = st[0].reshape(nbp, 1, N_HEADS, HEAD, HEAD)
    new_b = st[1].reshape(nbp, 1, N_HEADS, HEAD, HEAD)
    return (y_prompt.reshape(nbp, seq_p, D_MODEL), y_sample.reshape(nbs, seq_s, D_MODEL), new_f, new_b)
```

```python
import functools

import jax
import jax.numpy as jnp
from jax import lax
from jax.experimental import pallas as pl
from jax.experimental.pallas import tpu as pltpu

F32 = jnp.float32
BF16 = jnp.bfloat16

D_MODEL = 2048
D_A = 2048
D_B = 2048
HEAD = 64
N_HEADS = D_A // HEAD
R_LORA = 96
R_PAD = 128
N_DIR = 2
GRID_W = 64
RMS_EPS = 1e-6
GN_EPS = HEAD * 1e-5
N_RKV = 3 * D_A
N_LORA = 2 * N_DIR * R_LORA
N_SHIFTED = N_RKV + N_LORA
N_REST = D_A + 4 * D_B + 2 * D_MODEL

CHUNK = 64
INV_BASE = 4
G_HEADS = 4
GC = G_HEADS * HEAD
N_GROUPS = N_HEADS // G_HEADS
WKV_ROWS = 256

V7X_VMEM_LIMIT = 60 * 1024 * 1024


def _cparams(sem):
    return pltpu.CompilerParams(dimension_semantics=sem, vmem_limit_bytes=V7X_VMEM_LIMIT)


def _bdot(a, b):
    return jnp.dot(a.astype(BF16), b.astype(BF16), preferred_element_type=F32)


def _bdot_nt(a, b):
    return lax.dot_general(a.astype(BF16), b.astype(BF16), (((1,), (1,)), ((), ())),
                           preferred_element_type=F32)


def _bdot_tn(a, b):
    return lax.dot_general(a.astype(BF16), b.astype(BF16), (((0,), (0,)), ((), ())),
                           preferred_element_type=F32)


def _split3(x):
    h1 = x.astype(BF16)
    r1 = x - h1.astype(F32)
    h2 = r1.astype(BF16)
    h3 = (r1 - h2.astype(F32)).astype(BF16)
    return h1, h2, h3


def _dot_exact_rhs(x, m16):
    h1, h2, h3 = _split3(x)
    d = lambda a: jnp.dot(a, m16, preferred_element_type=F32)
    return d(h1) + d(h2) + d(h3)


def _sigmoid(x):
    return 1.0 / (1.0 + jnp.exp(-x))


EXP_M05 = 0.6065306597126334


def _ada_kernel(c_ref, w_ref, b_ref, o_ref):
    c = c_ref[...]
    s = c * _sigmoid(c)
    o_ref[...] = jnp.dot(s, w_ref[...], precision=lax.Precision.HIGHEST,
                         preferred_element_type=F32) + b_ref[...]


def _ada_call(cvec, ada_w, ada_b):
    n = ada_w.shape[1]
    tn = 1024
    return pl.pallas_call(
        _ada_kernel,
        grid=(n // tn,),
        in_specs=[pl.BlockSpec((8, D_MODEL), lambda j: (0, 0)),
                  pl.BlockSpec((D_MODEL, tn), lambda j: (0, j)),
                  pl.BlockSpec((1, tn), lambda j: (0, j))],
        out_specs=pl.BlockSpec((8, tn), lambda j: (0, j)),
        out_shape=jax.ShapeDtypeStruct((8, n), F32),
        compiler_params=_cparams(("arbitrary",)),
        name="ada",
    )(cvec, ada_w, ada_b)


NORM_TM = 512
NORM_ROWS = 128


def _norm_kernel(xc_ref, xl_ref, sh_ref, sc_ref, g_ref, o_ref, *, n_ctx_tiles):
    g = g_ref[...]
    sc = 1.0 + sc_ref[...]
    sh = sh_ref[...]

    def run(x_ref):
        def body(i, carry):
            rows = pl.ds(pl.multiple_of(i * NORM_ROWS, NORM_ROWS), NORM_ROWS)
            x = x_ref[rows, :]
            ms = jnp.mean(x * x, axis=-1, keepdims=True)
            xn = x * lax.rsqrt(ms + RMS_EPS) * g
            o_ref[rows, :] = (xn * sc + sh).astype(BF16)
            return carry

        lax.fori_loop(0, NORM_TM // NORM_ROWS, body, 0)

    @pl.when(pl.program_id(0) < n_ctx_tiles)
    def _():
        run(xc_ref)

    @pl.when(pl.program_id(0) >= n_ctx_tiles)
    def _():
        run(xl_ref)


def _norm_call(xc, xl, mod, norm_g, seq_lat, ctx_row):
    mc, ml = xc.shape[0], xl.shape[0]
    assert mc % NORM_TM == 0 and ml % NORM_TM == 0 and seq_lat % NORM_TM == 0
    ntc = mc // NORM_TM
    tiles_per_seq = seq_lat // NORM_TM
    row = lambda i: jnp.where(i < ntc, ctx_row, (i - ntc) // tiles_per_seq)
    return pl.pallas_call(
        functools.partial(_norm_kernel, n_ctx_tiles=ntc),
        grid=((mc + ml) // NORM_TM,),
        in_specs=[pl.BlockSpec((NORM_TM, D_MODEL), lambda i: (jnp.minimum(i, ntc - 1), 0)),
                  pl.BlockSpec((NORM_TM, D_MODEL), lambda i: (jnp.maximum(i - ntc, 0), 0)),
                  pl.BlockSpec((None, 1, D_MODEL), lambda i: (row(i), 0, 0)),
                  pl.BlockSpec((None, 1, D_MODEL), lambda i: (row(i), 0, 1)),
                  pl.BlockSpec((1, D_MODEL), lambda i: (0, 0))],
        out_specs=pl.BlockSpec((NORM_TM, D_MODEL), lambda i: (i, 0)),
        out_shape=jax.ShapeDtypeStruct((mc + ml, D_MODEL), BF16),
        compiler_params=_cparams(("arbitrary",)),
        name="norm",
    )(xc, xl, mod, mod, norm_g)


IN_TM = 1024
IN_TN = 512
LORA_TILE = N_RKV // IN_TN
Z_LORA_COL = N_RKV
ZS_COLS = N_RKV + IN_TN


REST_TM = 2048


def _inproj_rest_kernel(h_ref, w_ref, o_ref):
    rows = pl.ds(pl.multiple_of(pl.program_id(1) * REST_TM, REST_TM), REST_TM)
    o_ref[...] = jnp.dot(h_ref[rows, :], w_ref[...].astype(BF16), preferred_element_type=F32)


def _inproj_rest_call(h, w_in):
    m = h.shape[0]
    assert m % REST_TM == 0 and N_REST % IN_TN == 0 and N_SHIFTED % 128 == 0
    return pl.pallas_call(
        _inproj_rest_kernel,
        grid=(N_REST // IN_TN, m // REST_TM),
        in_specs=[pl.BlockSpec((m, D_MODEL), lambda j, i: (0, 0), pipeline_mode=pl.Buffered(1)),
                  pl.BlockSpec((pl.Element(D_MODEL), pl.Element(IN_TN)),
                               lambda j, i: (0, pl.multiple_of(N_SHIFTED + j * IN_TN, 128)))],
        out_specs=pl.BlockSpec((REST_TM, IN_TN), lambda j, i: (i, j)),
        out_shape=jax.ShapeDtypeStruct((m, N_REST), F32),
        compiler_params=_cparams(("arbitrary", "arbitrary")),
        name="inproj_rest",
    )(h, w_in)


def _inproj_shift_kernel(h_ref, w_ref, wl_ref, mup_ref, mun_ref, o_ref, *, n_ctx_tiles, seq_ctx, seq_lat):
    j = pl.program_id(0)
    i = pl.program_id(1)
    rows = pl.ds(pl.multiple_of(i * IN_TM, IN_TM), IN_TM)
    w = jnp.where(j == LORA_TILE, wl_ref[...], w_ref[...]).astype(BF16)
    z = jnp.dot(h_ref[rows, :], w, preferred_element_type=F32)
    seq_len = jnp.where(i < n_ctx_tiles, seq_ctx, seq_lat)
    pos = lax.broadcasted_iota(jnp.int32, z.shape, 0) & (seq_len - 1)
    zp = jnp.where(pos == 0, 0.0, pltpu.roll(z, 1, 0))
    zn = jnp.where(pos == seq_len - 1, 0.0, pltpu.roll(z, IN_TM - 1, 0))
    o_ref[...] = z + mup_ref[...] * (zp - z) + mun_ref[...] * (zn - z)


def _inproj_shift_call(h, w_in, w_lora, mu_prev, mu_next, m_ctx, seq_ctx, seq_lat):
    m = h.shape[0]
    assert m % IN_TM == 0 and m_ctx % IN_TM == 0 and IN_TM % seq_ctx == 0 and IN_TM % seq_lat == 0
    w_col = lambda j: pl.multiple_of(jnp.minimum(j, LORA_TILE - 1) * IN_TN, 128)
    resident = dict(pipeline_mode=pl.Buffered(1))
    return pl.pallas_call(
        functools.partial(_inproj_shift_kernel, n_ctx_tiles=m_ctx // IN_TM, seq_ctx=seq_ctx, seq_lat=seq_lat),
        grid=(LORA_TILE + 1, m // IN_TM),
        in_specs=[pl.BlockSpec((m, D_MODEL), lambda j, i: (0, 0), **resident),
                  pl.BlockSpec((pl.Element(D_MODEL), pl.Element(IN_TN)), lambda j, i: (0, w_col(j))),
                  pl.BlockSpec((D_MODEL, IN_TN), lambda j, i: (0, 0), **resident),
                  pl.BlockSpec((1, IN_TN), lambda j, i: (0, j)),
                  pl.BlockSpec((1, IN_TN), lambda j, i: (0, j))],
        out_specs=pl.BlockSpec((IN_TM, IN_TN), lambda j, i: (i, j)),
        out_shape=jax.ShapeDtypeStruct((m, ZS_COLS), F32),
        compiler_params=_cparams(("arbitrary", "arbitrary")),
        name="inproj_shift",
    )(h, w_in, w_lora, mu_prev, mu_next)


WKV_BLK = 1024
N_STAGE = 4
STAGE_CHUNKS = WKV_BLK // CHUNK // N_STAGE


def _lockstep(tasks):
    tasks = list(tasks)
    while tasks:
        alive = []
        for t in tasks:
            try:
                next(t)
                alive.append(t)
            except StopIteration:
                pass
        tasks = alive


def _wkv_kernel(*refs, seq_len, has_s0, emit_state):
    it = iter(refs)
    r_ref, k_ref, v_ref, lora_ref, ga_ref, p8_ref, w0a0_ref, w2a2_ref = (next(it) for _ in range(8))
    s0_refs = (next(it), next(it)) if has_s0 else None
    ya_ref = next(it)
    st_refs = (next(it), next(it)) if emit_state else None
    (atrt_ref, arb_ref, bkt_ref, u0_ref, y0_ref, pct_ref, kk_ref, lw_ref, kd_ref, bb_ref,
     yacc_ref, bonus_ref, s_ref) = (next(it) for _ in range(13))

    n_seq = WKV_BLK // seq_len
    n_chunks = WKV_BLK // CHUNK
    p8 = p8_ref[...]
    k_k, k_a, r_k, lnx_g, lnx_b = (p8[i:i + 1] for i in range(5))
    w0a0 = w0a0_ref[...]

    row_c = lax.broadcasted_iota(jnp.int32, (CHUNK, GC), 0)
    col_c = lax.broadcasted_iota(jnp.int32, (CHUNK, GC), 1) & (HEAD - 1)
    strict = (col_c < row_c, col_c > row_c)
    incl = (col_c <= row_c, col_c >= row_c)
    eye_pk = jnp.where(col_c == row_c, 1.0, 0.0)
    same_blk = {}
    m = INV_BASE
    while m <= CHUNK:
        same_blk[m] = (row_c // m) == (col_c // m)
        m *= 2
    rb = lax.broadcasted_iota(jnp.int32, (GC, GC), 0) // HEAD
    cb = lax.broadcasted_iota(jnp.int32, (GC, GC), 1) // HEAD
    bd_mask = rb == cb
    ones_bd16 = jnp.where(bd_mask, 1.0, 0.0).astype(BF16)

    def bd16(x):
        t = jnp.concatenate([x.astype(F32)] * G_HEADS, axis=0)
        return jnp.where(bd_mask, t, 0.0).astype(BF16)

    def split2(x):
        h1 = x.astype(BF16)
        return h1, (x - h1.astype(F32)).astype(BF16)

    def ones_dot(x16):
        return jnp.dot(x16, ones_bd16, preferred_element_type=F32)

    def running_sum(x, d):
        s = 1
        while s < CHUNK:
            if d == 0:
                x = x + jnp.where(row_c >= s, pltpu.roll(x, s, 0), 0.0)
            else:
                x = x + jnp.where(row_c < CHUNK - s, pltpu.roll(x, CHUNK - s, 0), 0.0)
            s *= 2
        return x

    def prep(j):
        rows = pl.ds(j * WKV_ROWS, WKV_ROWS)
        r = r_ref[rows, :]
        k = k_ref[rows, :]
        lo = lora_ref[rows, :]
        kk = k * k_k
        q1, q2 = split2(kk * kk)
        ssq = ones_dot(q1)
        yield
        ssq = ssq + ones_dot(q2)
        yield
        kk = kk / jnp.maximum(jnp.sqrt(ssq), 1e-12)
        kk_ref[rows, :] = kk
        kd_sum = None
        for d in range(N_DIR):
            wd = jnp.tanh(lo[:, d * R_PAD:(d + 1) * R_PAD])
            ad = lo[:, (N_DIR + d) * R_PAD:(N_DIR + d + 1) * R_PAD]
            w_lin = w0a0[d:d + 1] + _bdot(wd, w2a2_ref[d])
            yield
            lw_ref[d, rows, :] = -EXP_M05 * _sigmoid(w_lin)
            a = _sigmoid(w0a0[N_DIR + d:N_DIR + d + 1] + _bdot(ad, w2a2_ref[N_DIR + d]))
            yield
            kd = k * (1.0 + (a - 1.0) * k_a)
            kd_ref[d, rows, :] = kd
            bb_ref[d, rows, :] = kk * a
            kd_sum = kd if kd_sum is None else kd_sum + kd
        b1, b2 = split2(r * kd_sum * r_k)
        bs = ones_dot(b1)
        yield
        bs = bs + ones_dot(b2)
        yield
        bonus_ref[rows, :] = bs * v_ref[rows, :]

    def a_job(c, d):
        rows = pl.ds(c * CHUNK, CHUNK)
        r = r_ref[rows, :]
        kk = kk_ref[rows, :]
        lw = lw_ref[d, rows, :]
        cs = running_sum(lw, d)
        e = jnp.exp(cs)
        e_inv = jnp.exp(-cs)
        rt = r * e
        at = -kk * jnp.exp(cs - lw)
        bt = bb_ref[d, rows, :] * e_inv
        kt = kd_ref[d, rows, :] * e_inv
        atrt_ref[d, c, CHUNK:2 * CHUNK, :] = rt.astype(BF16)
        bkt_ref[d, c] = jnp.concatenate([bt, kt], axis=0).T.astype(BF16)
        edge = e[CHUNK - 1:CHUNK] if d == 0 else e[0:1]
        pct_ref[d, c] = jnp.broadcast_to(edge, (GC // 2, GC)).T
        ar = jnp.concatenate([at, rt], axis=0).astype(BF16)
        sb = _bdot_nt(ar, bd16(bt))
        yield
        sk = _bdot_nt(ar, bd16(kt))
        yield
        l_ab = jnp.where(strict[d], sb[:CHUNK], 0.0)
        arb_ref[d, c] = jnp.where(incl[d], sb[CHUNK:], 0.0).astype(BF16)
        lak_ark = jnp.concatenate([jnp.where(strict[d], sk[:CHUNK], 0.0),
                                   jnp.where(incl[d], sk[CHUNK:], 0.0)], axis=0).astype(BF16)
        l_base = jnp.where(same_blk[INV_BASE], l_ab, 0.0)
        l2 = _bdot(l_base, bd16(l_base))
        yield
        tm = eye_pk + l_base
        tm = tm + _bdot(tm, bd16(l2))
        yield
        m = INV_BASE
        while m < CHUNK:
            off = same_blk[2 * m] & jnp.logical_not(same_blk[m])
            m1 = _bdot(tm, bd16(jnp.where(off, l_ab, 0.0)))
            yield
            tm = tm + _bdot(m1, bd16(tm))
            yield
            m *= 2
        atrt_ref[d, c, 0:CHUNK, :] = _bdot(tm, bd16(at)).astype(BF16)
        yield
        xv = _bdot(lak_ark, bd16(v_ref[rows, :]))
        yield
        y0_ref[d, c] = xv[CHUNK:]
        u0_ref[d, c] = _bdot(tm, bd16(xv[:CHUNK]))
        yield

    def b_chain(d, chunks, init, out_seq):
        if init == "zero":
            s_ref[d] = jnp.zeros((GC, GC), F32)
        elif init == "given":
            tile16 = jnp.where(lax.broadcasted_iota(jnp.int32, (HEAD, GC), 0)
                               == (lax.broadcasted_iota(jnp.int32, (HEAD, GC), 1) & (HEAD - 1)),
                               1.0, 0.0).astype(BF16)
            s_ref[d] = jnp.where(bd_mask, _dot_exact_rhs(s0_refs[d][0], tile16), 0.0).T
            yield
        for c in chunks:
            rows = pl.ds(c * CHUNK, CHUNK)
            s = s_ref[d]
            us = _bdot(atrt_ref[d, c], s.astype(BF16))
            yield
            u = us[:CHUNK] + u0_ref[d, c]
            uv = jnp.concatenate([u.astype(BF16), v_ref[rows, :].astype(BF16)], axis=0)
            ds = _bdot(bkt_ref[d, c], uv)
            yield
            pcc = pct_ref[d, c]
            s_ref[d] = (s + jnp.where(bd_mask, ds, 0.0)) * jnp.concatenate([pcc, pcc], axis=1)
            y = us[CHUNK:] + _bdot(arb_ref[d, c], bd16(u)) + y0_ref[d, c]
            yield
            yacc_ref[rows, :] += y
        if out_seq is not None:
            untile16 = jnp.where((lax.broadcasted_iota(jnp.int32, (GC, HEAD), 0) & (HEAD - 1))
                                 == lax.broadcasted_iota(jnp.int32, (GC, HEAD), 1), 1.0, 0.0).astype(BF16)
            st_refs[d][out_seq] = _dot_exact_rhs(s_ref[d].T, untile16)
            yield

    def post(j):
        rows = pl.ds(j * WKV_ROWS, WKV_ROWS)
        y = yacc_ref[rows, :]
        y1, y2 = split2(y)
        mu = ones_dot(y1)
        yield
        mu = (mu + ones_dot(y2)) * (1.0 / HEAD)
        yield
        yc = y - mu
        v1, v2 = split2(yc * yc)
        var = ones_dot(v1)
        yield
        var = (var + ones_dot(v2)) * (1.0 / HEAD)
        yield
        out = yc * lax.rsqrt(var + GN_EPS) * lnx_g + lnx_b + bonus_ref[rows, :]
        ga = ga_ref[rows, :]
        ya_ref[rows, :] = (out * (ga * _sigmoid(ga))).astype(BF16)

    yacc_ref[...] = jnp.zeros_like(yacc_ref)
    sc = STAGE_CHUNKS
    if n_seq == 1:
        fwd = [[i * sc + q for q in range(sc)] for i in range(N_STAGE)]
        bwd = [[n_chunks - 1 - i * sc - q for q in range(sc)] for i in range(N_STAGE)]
        a_jobs = [[a_job(c, 0) for c in fwd[i]] + [a_job(c, 1) for c in bwd[i]] for i in range(N_STAGE)]
        init = lambda i: ("given" if has_s0 else "zero") if i == 0 else None
        chains = [[b_chain(0, fwd[i], init(i), None), b_chain(1, bwd[i], init(i), None)] for i in range(N_STAGE)]
        stages = [
            ([], [prep(0), prep(N_STAGE - 1)]),
            (a_jobs[0], [prep(j) for j in range(1, N_STAGE - 1)]),
        ]
        for i in range(1, N_STAGE):
            stages.append((a_jobs[i], chains[i - 1]))
        stages.append(([], chains[N_STAGE - 1] + [post(j) for j in range(1, N_STAGE - 1)]))
        stages.append(([], [post(0), post(N_STAGE - 1)]))
    else:
        assert n_seq == N_STAGE and seq_len == sc * CHUNK
        fwd = [[i * sc + q for q in range(sc)] for i in range(N_STAGE)]
        bwd = [list(reversed(f)) for f in fwd]
        a_jobs = [[a_job(c, 0) for c in fwd[i]] + [a_job(c, 1) for c in bwd[i]] for i in range(N_STAGE)]
        out_seq = (lambda i: i) if emit_state else (lambda i: None)
        chains = [[b_chain(0, fwd[i], "zero", out_seq(i)), b_chain(1, bwd[i], "zero", out_seq(i))]
                  for i in range(N_STAGE)]
        stages = [([], [prep(0)]), (a_jobs[0], [prep(1)])]
        for i in range(1, N_STAGE):
            side = list(chains[i - 1])
            if i + 1 < N_STAGE:
                side.append(prep(i + 1))
            if i >= 2:
                side.append(post(i - 2))
            stages.append((a_jobs[i], side))
        stages.append(([], chains[N_STAGE - 1] + [post(N_STAGE - 2)]))
        stages.append(([], [post(N_STAGE - 1)]))
    for jobs, side in stages:
        _lockstep(jobs + side)


def _wkv_call(zs, z, row0, m, p8, w0a0, w2a2, seq_len, s0=None, emit_state=False, name="wkv"):
    assert row0 % WKV_BLK == 0 and m % WKV_BLK == 0 and WKV_BLK % seq_len == 0
    n_seq = WKV_BLK // seq_len
    nb = m // WKV_BLK
    b0 = row0 // WKV_BLK
    nc = WKV_BLK // CHUNK
    has_s0 = s0 is not None
    assert not has_s0 or n_seq == 1
    ncb = D_A // GC
    lora_w = 2 * N_DIR * R_PAD
    in_specs = [
        pl.BlockSpec((WKV_BLK, GC), lambda b, g: (b0 + b, g)),
        pl.BlockSpec((WKV_BLK, GC), lambda b, g: (b0 + b, ncb + g)),
        pl.BlockSpec((WKV_BLK, GC), lambda b, g: (b0 + b, 2 * ncb + g)),
        pl.BlockSpec((WKV_BLK, lora_w), lambda b, g: (b0 + b, Z_LORA_COL // lora_w)),
        pl.BlockSpec((WKV_BLK, GC), lambda b, g: (b0 + b, g)),
        pl.BlockSpec((8, GC), lambda b, g: (0, g)),
        pl.BlockSpec((2 * N_DIR, GC), lambda b, g: (0, g)),
        pl.BlockSpec((2 * N_DIR, R_PAD, GC), lambda b, g: (0, 0, g)),
    ]
    args = [zs, zs, zs, zs, z, p8, w0a0, w2a2]
    if has_s0:
        in_specs += [pl.BlockSpec((1, GC, HEAD), lambda b, g: (b, g, 0))] * 2
        args += list(s0)
    out_specs = [pl.BlockSpec((WKV_BLK, GC), lambda b, g: (b, g))]
    out_shape = [jax.ShapeDtypeStruct((m, D_A), BF16)]
    if emit_state:
        out_specs += [pl.BlockSpec((n_seq, GC, HEAD), lambda b, g: (b, g, 0))] * 2
        out_shape += [jax.ShapeDtypeStruct((nb * n_seq, D_A, HEAD), F32)] * 2
    pair16 = pltpu.VMEM((N_DIR, nc, 2 * CHUNK, GC), BF16)
    big16 = pltpu.VMEM((N_DIR, nc, CHUNK, GC), BF16)
    big32 = pltpu.VMEM((N_DIR, nc, CHUNK, GC), F32)
    seq32 = pltpu.VMEM((WKV_BLK, GC), F32)
    dir32 = pltpu.VMEM((N_DIR, WKV_BLK, GC), F32)
    scratch = [pair16, big16, pltpu.VMEM((N_DIR, nc, GC, 2 * CHUNK), BF16), big32, big32,
               pltpu.VMEM((N_DIR, nc, GC, GC // 2), F32),
               seq32, dir32, dir32, dir32,
               seq32, seq32,
               pltpu.VMEM((N_DIR, GC, GC), F32)]
    return pl.pallas_call(
        functools.partial(_wkv_kernel, seq_len=seq_len, has_s0=has_s0, emit_state=emit_state),
        grid=(nb, N_GROUPS),
        in_specs=in_specs,
        out_specs=out_specs,
        out_shape=out_shape,
        scratch_shapes=scratch,
        compiler_params=_cparams(("arbitrary", "arbitrary")),
        name=name,
    )(*args)


CONV_TM = 256
CONV_TN = 1024


def _conv_kernel(b_ref, c_ref, x_ref, g_ref, w_ref, o_ref, *, row_len):
    p = c_ref[...] * x_ref[...]
    pos = lax.broadcasted_iota(jnp.int32, p.shape, 0) & (row_len - 1)
    pp = jnp.where(pos == 0, 0.0, pltpu.roll(p, 1, 0))
    pn = jnp.where(pos == row_len - 1, 0.0, pltpu.roll(p, CONV_TM - 1, 0))
    w = w_ref[...]
    u = w[0:1] * pp + w[1:2] * p + w[2:3] * pn
    g = g_ref[...]
    o_ref[...] = (b_ref[...] * u * (g * _sigmoid(g))).astype(BF16)


def _conv_call(z, row0, m, conv_w, row_len, name="conv"):
    nb = D_B // CONV_TN
    off = D_A // CONV_TN
    i0 = row0 // CONV_TM
    spec = lambda q: pl.BlockSpec((CONV_TM, CONV_TN), lambda i, j: (i0 + i, off + q * nb + j))
    return pl.pallas_call(
        functools.partial(_conv_kernel, row_len=row_len),
        grid=(m // CONV_TM, nb),
        in_specs=[spec(0), spec(1), spec(2), spec(3),
                  pl.BlockSpec((3, CONV_TN), lambda i, j: (0, j))],
        out_specs=pl.BlockSpec((CONV_TM, CONV_TN), lambda i, j: (i, j)),
        out_shape=jax.ShapeDtypeStruct((m, D_B), BF16),
        compiler_params=_cparams(("arbitrary", "arbitrary")),
        name=name,
    )(z, z, z, z, conv_w)


OUT_TM = 256


def _outproj_kernel(ya_ref, yb_ref, ma_ref, mb_ref, x_ref, gate_ref, fg_ref, wa_ref, wb_ref, wo_ref, o_ref):
    y_a = jnp.dot(ya_ref[...], wa_ref[...], preferred_element_type=F32)
    y_b = jnp.dot(yb_ref[...], wb_ref[...], preferred_element_type=F32)
    merged = _sigmoid(ma_ref[...]) * y_a + _sigmoid(mb_ref[...]) * y_b
    out = jnp.dot(merged.astype(BF16), wo_ref[...], preferred_element_type=F32)
    xo = x_ref[...] + gate_ref[...] * out
    ms = jnp.mean(xo * xo, axis=-1, keepdims=True)
    o_ref[...] = xo * lax.rsqrt(ms + RMS_EPS) * fg_ref[...]


def _outproj_call(ya, yb, z, row0, x, mod, final_g, wa16, wb16, wo16, rows_per_mod, mod_row0, mod_row_step,
                  name="outproj"):
    m = x.shape[0]
    mcol = (D_A + 4 * D_B) // D_MODEL
    i0 = row0 // OUT_TM
    row = lambda i: mod_row0 + mod_row_step * (i // rows_per_mod)
    full = lambda: pl.BlockSpec((D_MODEL, D_MODEL), lambda i: (0, 0), pipeline_mode=pl.Buffered(1))
    tile = lambda col: pl.BlockSpec((OUT_TM, D_MODEL), lambda i: (i, col))
    ztile = lambda col: pl.BlockSpec((OUT_TM, D_MODEL), lambda i: (i0 + i, col))
    return pl.pallas_call(
        _outproj_kernel,
        grid=(m // OUT_TM,),
        in_specs=[tile(0), tile(0), ztile(mcol), ztile(mcol + 1), tile(0),
                  pl.BlockSpec((None, 1, D_MODEL), lambda i: (row(i), 0, 2)),
                  pl.BlockSpec((1, D_MODEL), lambda i: (0, 0)),
                  full(), full(), full()],
        out_specs=tile(0),
        out_shape=jax.ShapeDtypeStruct((m, D_MODEL), F32),
        compiler_params=_cparams(("arbitrary",)),
        name=name,
    )(ya, yb, z, z, x, mod, final_g, wa16, wb16, wo16)


def _pad_lora(a, axis):
    shp = a.shape
    a = a.reshape(shp[:axis] + (2 * N_DIR, R_LORA) + shp[axis + 1:])
    pad = [(0, 0)] * a.ndim
    pad[axis + 1] = (0, R_PAD - R_LORA)
    a = jnp.pad(a, pad)
    return a.reshape(shp[:axis] + (2 * N_DIR * R_PAD,) + shp[axis + 1:])


def _stream(x, zs, z, row0, seq_len, mod, mod_row0, mod_row_step, conv_rows, lw, s0, emit_state, tag):
    (p8, w0a0, w2a2, conv_w, wa16, wb16, wo16, final_g) = lw
    m = x.shape[0]
    wkv_out = _wkv_call(zs, z, row0, m, p8, w0a0, w2a2, seq_len, s0=s0, emit_state=emit_state, name="wkv_" + tag)
    ya = wkv_out[0]
    yb = _conv_call(z, row0, m, conv_w, seq_len // conv_rows, name="conv_" + tag)
    y = _outproj_call(ya, yb, z, row0, x, mod, final_g, wa16, wb16, wo16, seq_len // OUT_TM,
                      mod_row0, mod_row_step, name="outproj_" + tag)
    return y, wkv_out[1:]


def kernel(x_prompt, x_sample, state_wkv_fwd, state_wkv_bwd, c, c_ctx, ada_w, ada_b, norm_g, w_in, mu_prev,
           mu_next, w0, w2, a0, a2, k_k, k_a, r_k, lnx_g, lnx_b, conv_w, w_out_a, w_out_b, w_o, final_g):
    depth = norm_g.shape[0]
    assert depth == 1
    l = 0
    nbp, seq_p, _ = x_prompt.shape
    nbs, seq_s, _ = x_sample.shape
    lat_rows = seq_s // GRID_W
    xc = x_prompt.reshape(nbp * seq_p, D_MODEL)
    xl = x_sample.reshape(nbs * seq_s, D_MODEL)

    cvec = jnp.concatenate([c, c_ctx[None, :], jnp.zeros((8 - nbs - 1, D_MODEL), F32)], axis=0)
    mod = _ada_call(cvec, ada_w[l], ada_b[l][None, :]).reshape(8, 1, 3 * D_MODEL)

    w_lora = _pad_lora(w_in[l][:, N_RKV:N_SHIFTED], 1)
    mu_p = jnp.concatenate([mu_prev[l][None, :N_RKV], _pad_lora(mu_prev[l][None, N_RKV:], 1)], axis=1)
    mu_n = jnp.concatenate([mu_next[l][None, :N_RKV], _pad_lora(mu_next[l][None, N_RKV:], 1)], axis=1)
    zrow = jnp.zeros((D_A,), F32)
    p8 = jnp.stack([k_k[l], k_a[l], r_k[l].reshape(D_A), lnx_g[l], lnx_b[l], zrow, zrow, zrow], axis=0)
    w0a0 = jnp.concatenate([w0[l], a0[l]], axis=0)
    w2a2 = jnp.pad(jnp.concatenate([w2[l], a2[l]], axis=0), ((0, 0), (0, R_PAD - R_LORA), (0, 0)))
    lw = (p8, w0a0, w2a2, conv_w[l],
          w_out_a[l].astype(BF16), w_out_b[l].astype(BF16), w_o[l].astype(BF16), final_g[None, :])

    h = _norm_call(xc, xl, mod, norm_g[l][None, :], seq_s, nbs)
    zs = _inproj_shift_call(h, w_in[l], w_lora, mu_p, mu_n, xc.shape[0], seq_p, seq_s)
    z = _inproj_rest_call(h, w_in[l])

    y_prompt, st = _stream(xc, zs, z, 0, seq_p, mod, nbs, 0, 1, lw, None, True, "ctx")
    s0 = (state_wkv_fwd[:, l].reshape(nbs, D_A, HEAD), state_wkv_bwd[:, l].reshape(nbs, D_A, HEAD))
    y_sample, _ = _stream(xl, zs, z, xc.shape[0], seq_s, mod, 0, 1, lat_rows, lw, s0, False, "lat")
    new_f = st[0].reshape(nbp, 1, N_HEADS, HEAD, HEAD)
    new_b = st[1].reshape(nbp, 1, N_HEADS, HEAD, HEAD)
    return (y_prompt.reshape(nbp, seq_p, D_MODEL), y_sample.reshape(nbs, seq_s, D_MODEL), new_f, new_b)
```

```python
import functools

import jax
import jax.numpy as jnp
from jax import lax
from jax.experimental import pallas as pl
from jax.experimental.pallas import tpu as pltpu

F32 = jnp.float32
BF16 = jnp.bfloat16

D_MODEL = 2048
D_A = 2048
D_B = 2048
HEAD = 64
N_HEADS = D_A // HEAD
R_LORA = 96
R_PAD = 128
N_DIR = 2
GRID_W = 64
RMS_EPS = 1e-6
GN_EPS = HEAD * 1e-5
N_RKV = 3 * D_A
N_LORA = 2 * N_DIR * R_LORA
N_SHIFTED = N_RKV + N_LORA
N_REST = D_A + 4 * D_B + 2 * D_MODEL

CHUNK = 64
INV_BASE = 4
G_HEADS = 4
GC = G_HEADS * HEAD
N_GROUPS = N_HEADS // G_HEADS
WKV_ROWS = 256

V7X_VMEM_LIMIT = 60 * 1024 * 1024


def _cparams(sem):
    return pltpu.CompilerParams(dimension_semantics=sem, vmem_limit_bytes=V7X_VMEM_LIMIT)


def _bdot(a, b):
    return jnp.dot(a.astype(BF16), b.astype(BF16), preferred_element_type=F32)


def _bdot_nt(a, b):
    return lax.dot_general(a.astype(BF16), b.astype(BF16), (((1,), (1,)), ((), ())),
                           preferred_element_type=F32)


def _bdot_tn(a, b):
    return lax.dot_general(a.astype(BF16), b.astype(BF16), (((0,), (0,)), ((), ())),
                           preferred_element_type=F32)


def _split3(x):
    h1 = x.astype(BF16)
    r1 = x - h1.astype(F32)
    h2 = r1.astype(BF16)
    h3 = (r1 - h2.astype(F32)).astype(BF16)
    return h1, h2, h3


def _dot_exact_rhs(x, m16):
    h1, h2, h3 = _split3(x)
    d = lambda a: jnp.dot(a, m16, preferred_element_type=F32)
    return d(h1) + d(h2) + d(h3)


def _sigmoid(x):
    return 1.0 / (1.0 + jnp.exp(-x))


EXP_M05 = 0.6065306597126334


def _ada_kernel(c_ref, w_ref, b_ref, o_ref):
    c = c_ref[...]
    s = c * _sigmoid(c)
    o_ref[...] = jnp.dot(s, w_ref[...], precision=lax.Precision.HIGHEST,
                         preferred_element_type=F32) + b_ref[...]


def _ada_call(cvec, ada_w, ada_b):
    n = ada_w.shape[1]
    tn = 1024
    return pl.pallas_call(
        _ada_kernel,
        grid=(n // tn,),
        in_specs=[pl.BlockSpec((8, D_MODEL), lambda j: (0, 0)),
                  pl.BlockSpec((D_MODEL, tn), lambda j: (0, j)),
                  pl.BlockSpec((1, tn), lambda j: (0, j))],
        out_specs=pl.BlockSpec((8, tn), lambda j: (0, j)),
        out_shape=jax.ShapeDtypeStruct((8, n), F32),
        compiler_params=_cparams(("arbitrary",)),
        name="ada",
    )(cvec, ada_w, ada_b)


NORM_TM = 512
NORM_ROWS = 128


def _norm_kernel(xc_ref, xl_ref, sh_ref, sc_ref, g_ref, o_ref, *, n_ctx_tiles):
    g = g_ref[...]
    sc = 1.0 + sc_ref[...]
    sh = sh_ref[...]

    def run(x_ref):
        def body(i, carry):
            rows = pl.ds(pl.multiple_of(i * NORM_ROWS, NORM_ROWS), NORM_ROWS)
            x = x_ref[rows, :]
            ms = jnp.mean(x * x, axis=-1, keepdims=True)
            xn = x * lax.rsqrt(ms + RMS_EPS) * g
            o_ref[rows, :] = (xn * sc + sh).astype(BF16)
            return carry

        lax.fori_loop(0, NORM_TM // NORM_ROWS, body, 0)

    @pl.when(pl.program_id(0) < n_ctx_tiles)
    def _():
        run(xc_ref)

    @pl.when(pl.program_id(0) >= n_ctx_tiles)
    def _():
        run(xl_ref)


def _norm_call(xc, xl, mod, norm_g, seq_lat, ctx_row):
    mc, ml = xc.shape[0], xl.shape[0]
    assert mc % NORM_TM == 0 and ml % NORM_TM == 0 and seq_lat % NORM_TM == 0
    ntc = mc // NORM_TM
    tiles_per_seq = seq_lat // NORM_TM
    row = lambda i: jnp.where(i < ntc, ctx_row, (i - ntc) // tiles_per_seq)
    return pl.pallas_call(
        functools.partial(_norm_kernel, n_ctx_tiles=ntc),
        grid=((mc + ml) // NORM_TM,),
        in_specs=[pl.BlockSpec((NORM_TM, D_MODEL), lambda i: (jnp.minimum(i, ntc - 1), 0)),
                  pl.BlockSpec((NORM_TM, D_MODEL), lambda i: (jnp.maximum(i - ntc, 0), 0)),
                  pl.BlockSpec((None, 1, D_MODEL), lambda i: (row(i), 0, 0)),
                  pl.BlockSpec((None, 1, D_MODEL), lambda i: (row(i), 0, 1)),
                  pl.BlockSpec((1, D_MODEL), lambda i: (0, 0))],
        out_specs=pl.BlockSpec((NORM_TM, D_MODEL), lambda i: (i, 0)),
        out_shape=jax.ShapeDtypeStruct((mc + ml, D_MODEL), BF16),
        compiler_params=_cparams(("arbitrary",)),
        name="norm",
    )(xc, xl, mod, mod, norm_g)


IN_TM = 1024
IN_TN = 512
LORA_TILE = N_RKV // IN_TN
Z_LORA_COL = N_RKV
ZS_COLS = N_RKV + IN_TN


REST_TM = 2048


def _inproj_rest_kernel(h_ref, w_ref, o_ref):
    rows = pl.ds(pl.multiple_of(pl.program_id(1) * REST_TM, REST_TM), REST_TM)
    o_ref[...] = jnp.dot(h_ref[rows, :], w_ref[...].astype(BF16), preferred_element_type=F32)


COL_GA = N_SHIFTED
COL_CONV = COL_GA + D_A
COL_MERGE = COL_CONV + 4 * D_B
N_GATES = D_A + 2 * D_MODEL


def _inproj_gates_call(h, w_in):
    m = h.shape[0]
    assert m % REST_TM == 0 and D_A % IN_TN == 0 and COL_GA % 128 == 0
    n_ga = D_A // IN_TN
    w_col = lambda j: pl.multiple_of(jnp.where(j < n_ga, COL_GA + j * IN_TN, COL_MERGE + (j - n_ga) * IN_TN), 128)
    return pl.pallas_call(
        _inproj_rest_kernel,
        grid=(N_GATES // IN_TN, m // REST_TM),
        in_specs=[pl.BlockSpec((m, D_MODEL), lambda j, i: (0, 0), pipeline_mode=pl.Buffered(1)),
                  pl.BlockSpec((pl.Element(D_MODEL), pl.Element(IN_TN)), lambda j, i: (0, w_col(j)))],
        out_specs=pl.BlockSpec((REST_TM, IN_TN), lambda j, i: (i, j)),
        out_shape=jax.ShapeDtypeStruct((m, N_GATES), F32),
        compiler_params=_cparams(("arbitrary", "arbitrary")),
        name="inproj_gates",
    )(h, w_in)


CONV_TN = 256


def _inproj_conv_kernel(h_ref, wb_ref, wc_ref, wx_ref, wg_ref, cw_ref, o_ref, *, n_ctx_tiles, row_ctx, row_lat):
    i = pl.program_id(1)
    rows = pl.ds(pl.multiple_of(i * IN_TM, IN_TM), IN_TM)
    proj = lambda w_ref: jnp.dot(h_ref[rows, :], w_ref[...].astype(BF16), preferred_element_type=F32)
    p = proj(wc_ref) * proj(wx_ref)
    row_len = jnp.where(i < n_ctx_tiles, row_ctx, row_lat)
    pos = lax.broadcasted_iota(jnp.int32, p.shape, 0) & (row_len - 1)
    pp = jnp.where(pos == 0, 0.0, pltpu.roll(p, 1, 0))
    pn = jnp.where(pos == row_len - 1, 0.0, pltpu.roll(p, IN_TM - 1, 0))
    w = cw_ref[...]
    u = w[0:1] * pp + w[1:2] * p + w[2:3] * pn
    g = proj(wg_ref)
    o_ref[...] = (proj(wb_ref) * u * (g * _sigmoid(g))).astype(BF16)


def _inproj_conv_call(h, w_in, conv_w, m_ctx, row_ctx, row_lat):
    m = h.shape[0]
    assert m % IN_TM == 0 and m_ctx % IN_TM == 0 and IN_TM % row_ctx == 0 and IN_TM % row_lat == 0
    wspec = lambda q: pl.BlockSpec((pl.Element(D_MODEL), pl.Element(CONV_TN)),
                                   lambda j, i: (0, pl.multiple_of(COL_CONV + q * D_B + j * CONV_TN, 128)))
    return pl.pallas_call(
        functools.partial(_inproj_conv_kernel, n_ctx_tiles=m_ctx // IN_TM, row_ctx=row_ctx, row_lat=row_lat),
        grid=(D_B // CONV_TN, m // IN_TM),
        in_specs=[pl.BlockSpec((m, D_MODEL), lambda j, i: (0, 0), pipeline_mode=pl.Buffered(1)),
                  wspec(0), wspec(1), wspec(2), wspec(3),
                  pl.BlockSpec((3, CONV_TN), lambda j, i: (0, j))],
        out_specs=pl.BlockSpec((IN_TM, CONV_TN), lambda j, i: (i, j)),
        out_shape=jax.ShapeDtypeStruct((m, D_B), BF16),
        compiler_params=_cparams(("arbitrary", "arbitrary")),
        name="inproj_conv",
    )(h, w_in, w_in, w_in, w_in, conv_w)


def _inproj_shift_kernel(h_ref, w_ref, wl_ref, mup_ref, mun_ref, o_ref, *, n_ctx_tiles, seq_ctx, seq_lat):
    j = pl.program_id(0)
    i = pl.program_id(1)
    rows = pl.ds(pl.multiple_of(i * IN_TM, IN_TM), IN_TM)
    w = jnp.where(j == LORA_TILE, wl_ref[...], w_ref[...]).astype(BF16)
    z = jnp.dot(h_ref[rows, :], w, preferred_element_type=F32)
    seq_len = jnp.where(i < n_ctx_tiles, seq_ctx, seq_lat)
    pos = lax.broadcasted_iota(jnp.int32, z.shape, 0) & (seq_len - 1)
    zp = jnp.where(pos == 0, 0.0, pltpu.roll(z, 1, 0))
    zn = jnp.where(pos == seq_len - 1, 0.0, pltpu.roll(z, IN_TM - 1, 0))
    o_ref[...] = z + mup_ref[...] * (zp - z) + mun_ref[...] * (zn - z)


def _inproj_shift_call(h, w_in, w_lora, mu_prev, mu_next, m_ctx, seq_ctx, seq_lat):
    m = h.shape[0]
    assert m % IN_TM == 0 and m_ctx % IN_TM == 0 and IN_TM % seq_ctx == 0 and IN_TM % seq_lat == 0
    w_col = lambda j: pl.multiple_of(jnp.minimum(j, LORA_TILE - 1) * IN_TN, 128)
    resident = dict(pipeline_mode=pl.Buffered(1))
    return pl.pallas_call(
        functools.partial(_inproj_shift_kernel, n_ctx_tiles=m_ctx // IN_TM, seq_ctx=seq_ctx, seq_lat=seq_lat),
        grid=(LORA_TILE + 1, m // IN_TM),
        in_specs=[pl.BlockSpec((m, D_MODEL), lambda j, i: (0, 0), **resident),
                  pl.BlockSpec((pl.Element(D_MODEL), pl.Element(IN_TN)), lambda j, i: (0, w_col(j))),
                  pl.BlockSpec((D_MODEL, IN_TN), lambda j, i: (0, 0), **resident),
                  pl.BlockSpec((1, IN_TN), lambda j, i: (0, j)),
                  pl.BlockSpec((1, IN_TN), lambda j, i: (0, j))],
        out_specs=pl.BlockSpec((IN_TM, IN_TN), lambda j, i: (i, j)),
        out_shape=jax.ShapeDtypeStruct((m, ZS_COLS), F32),
        compiler_params=_cparams(("arbitrary", "arbitrary")),
        name="inproj_shift",
    )(h, w_in, w_lora, mu_prev, mu_next)


WKV_BLK = 1024
N_STAGE = 4
STAGE_CHUNKS = WKV_BLK // CHUNK // N_STAGE


def _lockstep(tasks):
    tasks = list(tasks)
    while tasks:
        alive = []
        for t in tasks:
            try:
                next(t)
                alive.append(t)
            except StopIteration:
                pass
        tasks = alive


def _wkv_kernel(*refs, seq_len, has_s0, emit_state):
    it = iter(refs)
    r_ref, k_ref, v_ref, lora_ref, ga_ref, p8_ref, w0a0_ref, w2a2_ref = (next(it) for _ in range(8))
    s0_refs = (next(it), next(it)) if has_s0 else None
    ya_ref = next(it)
    st_refs = (next(it), next(it)) if emit_state else None
    (atrt_ref, arb_ref, bkt_ref, u0_ref, y0_ref, pct_ref, kk_ref, lw_ref, kd_ref, bb_ref,
     yacc_ref, bonus_ref, s_ref) = (next(it) for _ in range(13))

    n_seq = WKV_BLK // seq_len
    n_chunks = WKV_BLK // CHUNK
    p8 = p8_ref[...]
    k_k, k_a, r_k, lnx_g, lnx_b = (p8[i:i + 1] for i in range(5))
    w0a0 = w0a0_ref[...]

    row_c = lax.broadcasted_iota(jnp.int32, (CHUNK, GC), 0)
    col_c = lax.broadcasted_iota(jnp.int32, (CHUNK, GC), 1) & (HEAD - 1)
    strict = (col_c < row_c, col_c > row_c)
    incl = (col_c <= row_c, col_c >= row_c)
    eye_pk = jnp.where(col_c == row_c, 1.0, 0.0)
    same_blk = {}
    m = INV_BASE
    while m <= CHUNK:
        same_blk[m] = (row_c // m) == (col_c // m)
        m *= 2
    near = [{k: same_blk[INV_BASE] & (sg * (row_c - col_c) == k) for k in (2, 3)} for sg in (1, -1)]
    rb = lax.broadcasted_iota(jnp.int32, (GC, GC), 0) // HEAD
    cb = lax.broadcasted_iota(jnp.int32, (GC, GC), 1) // HEAD
    bd_mask = rb == cb
    ones_bd16 = jnp.where(bd_mask, 1.0, 0.0).astype(BF16)

    def bd16(x):
        t = jnp.concatenate([x.astype(F32)] * G_HEADS, axis=0)
        return jnp.where(bd_mask, t, 0.0).astype(BF16)

    def split2(x):
        h1 = x.astype(BF16)
        return h1, (x - h1.astype(F32)).astype(BF16)

    def ones_dot(x16):
        return jnp.dot(x16, ones_bd16, preferred_element_type=F32)

    def running_sum(x, d):
        s = 1
        while s < CHUNK:
            if d == 0:
                x = x + jnp.where(row_c >= s, pltpu.roll(x, s, 0), 0.0)
            else:
                x = x + jnp.where(row_c < CHUNK - s, pltpu.roll(x, CHUNK - s, 0), 0.0)
            s *= 2
        return x

    def prep(j):
        rows = pl.ds(j * WKV_ROWS, WKV_ROWS)
        r = r_ref[rows, :]
        k = k_ref[rows, :]
        lo = lora_ref[rows, :]
        kk = k * k_k
        q1, q2 = split2(kk * kk)
        ssq = ones_dot(q1)
        yield
        ssq = ssq + ones_dot(q2)
        yield
        kk = kk / jnp.maximum(jnp.sqrt(ssq), 1e-12)
        kk_ref[rows, :] = kk
        kd_sum = None
        for d in range(N_DIR):
            wd = jnp.tanh(lo[:, d * R_PAD:(d + 1) * R_PAD])
            ad = lo[:, (N_DIR + d) * R_PAD:(N_DIR + d + 1) * R_PAD]
            w_lin = w0a0[d:d + 1] + _bdot(wd, w2a2_ref[d])
            yield
            lw_ref[d, rows, :] = -EXP_M05 * _sigmoid(w_lin)
            a = _sigmoid(w0a0[N_DIR + d:N_DIR + d + 1] + _bdot(ad, w2a2_ref[N_DIR + d]))
            yield
            kd = k * (1.0 + (a - 1.0) * k_a)
            kd_ref[d, rows, :] = kd
            bb_ref[d, rows, :] = kk * a
            kd_sum = kd if kd_sum is None else kd_sum + kd
        b1, b2 = split2(r * kd_sum * r_k)
        bs = ones_dot(b1)
        yield
        bs = bs + ones_dot(b2)
        yield
        bonus_ref[rows, :] = bs * v_ref[rows, :]

    def a_job(c, d):
        rows = pl.ds(c * CHUNK, CHUNK)
        r = r_ref[rows, :]
        kk = kk_ref[rows, :]
        lw = lw_ref[d, rows, :]
        cs = running_sum(lw, d)
        e = jnp.exp(cs)
        e_inv = jnp.exp(-cs)
        rt = r * e
        at = -kk * jnp.exp(cs - lw)
        bt = bb_ref[d, rows, :] * e_inv
        kt = kd_ref[d, rows, :] * e_inv
        atrt_ref[d, c, CHUNK:2 * CHUNK, :] = rt.astype(BF16)
        bkt_ref[d, c] = jnp.concatenate([bt, kt], axis=0).T.astype(BF16)
        edge = e[CHUNK - 1:CHUNK] if d == 0 else e[0:1]
        pct_ref[d, c] = jnp.broadcast_to(edge, (GC // 2, GC)).T
        ar = jnp.concatenate([at, rt], axis=0).astype(BF16)
        sb = _bdot_nt(ar, bd16(bt))
        yield
        sk = _bdot_nt(ar, bd16(kt))
        yield
        l_ab = jnp.where(strict[d], sb[:CHUNK], 0.0)
        arb_ref[d, c] = jnp.where(incl[d], sb[CHUNK:], 0.0).astype(BF16)
        lak_ark = jnp.concatenate([jnp.where(strict[d], sk[:CHUNK], 0.0),
                                   jnp.where(incl[d], sk[CHUNK:], 0.0)], axis=0).astype(BF16)
        sgn = 1 if d == 0 else -1
        l4 = jnp.where(same_blk[INV_BASE], l_ab, 0.0)
        a1 = pltpu.roll(l4, (-sgn) % GC, 1)
        a2 = pltpu.roll(l4, (-2 * sgn) % GC, 1)
        t2 = l4 + jnp.where(near[d][2], a1 * pltpu.roll(l4, sgn % CHUNK, 0), 0.0)
        t3 = a1 * pltpu.roll(l4, (2 * sgn) % CHUNK, 0) + a2 * pltpu.roll(t2, sgn % CHUNK, 0)
        tm = eye_pk + t2 + jnp.where(near[d][3], t3, 0.0)
        m = INV_BASE
        while m < CHUNK:
            off = same_blk[2 * m] & jnp.logical_not(same_blk[m])
            m1 = _bdot(tm, bd16(jnp.where(off, l_ab, 0.0)))
            yield
            tm = tm + _bdot(m1, bd16(tm))
            yield
            m *= 2
        atrt_ref[d, c, 0:CHUNK, :] = _bdot(tm, bd16(at)).astype(BF16)
        yield
        xv = _bdot(lak_ark, bd16(v_ref[rows, :]))
        yield
        y0_ref[d, c] = xv[CHUNK:]
        u0_ref[d, c] = _bdot(tm, bd16(xv[:CHUNK]))
        yield

    def b_chain(d, chunks, init, out_seq):
        if init == "zero":
            s_ref[d] = jnp.zeros((GC, GC), F32)
        elif init == "given":
            tile16 = jnp.where(lax.broadcasted_iota(jnp.int32, (HEAD, GC), 0)
                               == (lax.broadcasted_iota(jnp.int32, (HEAD, GC), 1) & (HEAD - 1)),
                               1.0, 0.0).astype(BF16)
            s_ref[d] = jnp.where(bd_mask, _dot_exact_rhs(s0_refs[d][0], tile16), 0.0).T
            yield
        for c in chunks:
            rows = pl.ds(c * CHUNK, CHUNK)
            s = s_ref[d]
            us = _bdot(atrt_ref[d, c], s.astype(BF16))
            yield
            u = us[:CHUNK] + u0_ref[d, c]
            uv = jnp.concatenate([u.astype(BF16), v_ref[rows, :].astype(BF16)], axis=0)
            ds = _bdot(bkt_ref[d, c], uv)
            yield
            pcc = pct_ref[d, c]
            s_ref[d] = (s + jnp.where(bd_mask, ds, 0.0)) * jnp.concatenate([pcc, pcc], axis=1)
            y = us[CHUNK:] + _bdot(arb_ref[d, c], bd16(u)) + y0_ref[d, c]
            yield
            yacc_ref[rows, :] += y
        if out_seq is not None:
            untile16 = jnp.where((lax.broadcasted_iota(jnp.int32, (GC, HEAD), 0) & (HEAD - 1))
                                 == lax.broadcasted_iota(jnp.int32, (GC, HEAD), 1), 1.0, 0.0).astype(BF16)
            st_refs[d][out_seq] = _dot_exact_rhs(s_ref[d].T, untile16)
            yield

    def post(j):
        rows = pl.ds(j * WKV_ROWS, WKV_ROWS)
        y = yacc_ref[rows, :]
        y1, y2 = split2(y)
        mu = ones_dot(y1)
        yield
        mu = (mu + ones_dot(y2)) * (1.0 / HEAD)
        yield
        yc = y - mu
        v1, v2 = split2(yc * yc)
        var = ones_dot(v1)
        yield
        var = (var + ones_dot(v2)) * (1.0 / HEAD)
        yield
        out = yc * lax.rsqrt(var + GN_EPS) * lnx_g + lnx_b + bonus_ref[rows, :]
        ga = ga_ref[rows, :]
        ya_ref[rows, :] = (out * (ga * _sigmoid(ga))).astype(BF16)

    yacc_ref[...] = jnp.zeros_like(yacc_ref)
    sc = STAGE_CHUNKS
    if n_seq == 1:
        fwd = [[i * sc + q for q in range(sc)] for i in range(N_STAGE)]
        bwd = [[n_chunks - 1 - i * sc - q for q in range(sc)] for i in range(N_STAGE)]
        a_jobs = [[a_job(c, 0) for c in fwd[i]] + [a_job(c, 1) for c in bwd[i]] for i in range(N_STAGE)]
        init = lambda i: ("given" if has_s0 else "zero") if i == 0 else None
        chains = [[b_chain(0, fwd[i], init(i), None), b_chain(1, bwd[i], init(i), None)] for i in range(N_STAGE)]
        stages = [
            ([], [prep(0), prep(N_STAGE - 1)]),
            (a_jobs[0], [prep(j) for j in range(1, N_STAGE - 1)]),
        ]
        for i in range(1, N_STAGE):
            stages.append((a_jobs[i], chains[i - 1]))
        stages.append(([], chains[N_STAGE - 1] + [post(j) for j in range(1, N_STAGE - 1)]))
        stages.append(([], [post(0), post(N_STAGE - 1)]))
    else:
        assert n_seq == N_STAGE and seq_len == sc * CHUNK
        fwd = [[i * sc + q for q in range(sc)] for i in range(N_STAGE)]
        bwd = [list(reversed(f)) for f in fwd]
        a_jobs = [[a_job(c, 0) for c in fwd[i]] + [a_job(c, 1) for c in bwd[i]] for i in range(N_STAGE)]
        out_seq = (lambda i: i) if emit_state else (lambda i: None)
        chains = [[b_chain(0, fwd[i], "zero", out_seq(i)), b_chain(1, bwd[i], "zero", out_seq(i))]
                  for i in range(N_STAGE)]
        stages = [([], [prep(0)]), (a_jobs[0], [prep(1)])]
        for i in range(1, N_STAGE):
            side = list(chains[i - 1])
            if i + 1 < N_STAGE:
                side.append(prep(i + 1))
            if i >= 2:
                side.append(post(i - 2))
            stages.append((a_jobs[i], side))
        stages.append(([], chains[N_STAGE - 1] + [post(N_STAGE - 2)]))
        stages.append(([], [post(N_STAGE - 1)]))
    for jobs, side in stages:
        _lockstep(jobs + side)


def _wkv_call(zs, z, row0, m, p8, w0a0, w2a2, seq_len, s0=None, emit_state=False, name="wkv"):
    assert row0 % WKV_BLK == 0 and m % WKV_BLK == 0 and WKV_BLK % seq_len == 0
    n_seq = WKV_BLK // seq_len
    nb = m // WKV_BLK
    b0 = row0 // WKV_BLK
    nc = WKV_BLK // CHUNK
    has_s0 = s0 is not None
    assert not has_s0 or n_seq == 1
    ncb = D_A // GC
    lora_w = 2 * N_DIR * R_PAD
    in_specs = [
        pl.BlockSpec((WKV_BLK, GC), lambda b, g: (b0 + b, g)),
        pl.BlockSpec((WKV_BLK, GC), lambda b, g: (b0 + b, ncb + g)),
        pl.BlockSpec((WKV_BLK, GC), lambda b, g: (b0 + b, 2 * ncb + g)),
        pl.BlockSpec((WKV_BLK, lora_w), lambda b, g: (b0 + b, Z_LORA_COL // lora_w)),
        pl.BlockSpec((WKV_BLK, GC), lambda b, g: (b0 + b, g)),
        pl.BlockSpec((8, GC), lambda b, g: (0, g)),
        pl.BlockSpec((2 * N_DIR, GC), lambda b, g: (0, g)),
        pl.BlockSpec((2 * N_DIR, R_PAD, GC), lambda b, g: (0, 0, g)),
    ]
    args = [zs, zs, zs, zs, z, p8, w0a0, w2a2]
    if has_s0:
        in_specs += [pl.BlockSpec((1, GC, HEAD), lambda b, g: (b, g, 0))] * 2
        args += list(s0)
    out_specs = [pl.BlockSpec((WKV_BLK, GC), lambda b, g: (b, g))]
    out_shape = [jax.ShapeDtypeStruct((m, D_A), BF16)]
    if emit_state:
        out_specs += [pl.BlockSpec((n_seq, GC, HEAD), lambda b, g: (b, g, 0))] * 2
        out_shape += [jax.ShapeDtypeStruct((nb * n_seq, D_A, HEAD), F32)] * 2
    pair16 = pltpu.VMEM((N_DIR, nc, 2 * CHUNK, GC), BF16)
    big16 = pltpu.VMEM((N_DIR, nc, CHUNK, GC), BF16)
    big32 = pltpu.VMEM((N_DIR, nc, CHUNK, GC), F32)
    seq32 = pltpu.VMEM((WKV_BLK, GC), F32)
    dir32 = pltpu.VMEM((N_DIR, WKV_BLK, GC), F32)
    scratch = [pair16, big16, pltpu.VMEM((N_DIR, nc, GC, 2 * CHUNK), BF16), big32, big32,
               pltpu.VMEM((N_DIR, nc, GC, GC // 2), F32),
               seq32, dir32, dir32, dir32,
               seq32, seq32,
               pltpu.VMEM((N_DIR, GC, GC), F32)]
    return pl.pallas_call(
        functools.partial(_wkv_kernel, seq_len=seq_len, has_s0=has_s0, emit_state=emit_state),
        grid=(nb, N_GROUPS),
        in_specs=in_specs,
        out_specs=out_specs,
        out_shape=out_shape,
        scratch_shapes=scratch,
        compiler_params=_cparams(("arbitrary", "arbitrary")),
        name=name,
    )(*args)


OUT_TM = 256


def _outproj_kernel(ya_ref, yb_ref, ma_ref, mb_ref, x_ref, gate_ref, fg_ref, wa_ref, wb_ref, wo_ref, o_ref):
    y_a = jnp.dot(ya_ref[...], wa_ref[...], preferred_element_type=F32)
    y_b = jnp.dot(yb_ref[...], wb_ref[...], preferred_element_type=F32)
    merged = _sigmoid(ma_ref[...]) * y_a + _sigmoid(mb_ref[...]) * y_b
    out = jnp.dot(merged.astype(BF16), wo_ref[...], preferred_element_type=F32)
    xo = x_ref[...] + gate_ref[...] * out
    ms = jnp.mean(xo * xo, axis=-1, keepdims=True)
    o_ref[...] = xo * lax.rsqrt(ms + RMS_EPS) * fg_ref[...]


def _outproj_call(ya, yb, z, row0, x, mod, final_g, wa16, wb16, wo16, rows_per_mod, mod_row0, mod_row_step,
                  name="outproj"):
    m = x.shape[0]
    mcol = D_A // D_MODEL
    i0 = row0 // OUT_TM
    row = lambda i: mod_row0 + mod_row_step * (i // rows_per_mod)
    full = lambda: pl.BlockSpec((D_MODEL, D_MODEL), lambda i: (0, 0), pipeline_mode=pl.Buffered(1))
    tile = lambda col: pl.BlockSpec((OUT_TM, D_MODEL), lambda i: (i, col))
    ztile = lambda col: pl.BlockSpec((OUT_TM, D_MODEL), lambda i: (i0 + i, col))
    return pl.pallas_call(
        _outproj_kernel,
        grid=(m // OUT_TM,),
        in_specs=[tile(0), ztile(0), ztile(mcol), ztile(mcol + 1), tile(0),
                  pl.BlockSpec((None, 1, D_MODEL), lambda i: (row(i), 0, 2)),
                  pl.BlockSpec((1, D_MODEL), lambda i: (0, 0)),
                  full(), full(), full()],
        out_specs=tile(0),
        out_shape=jax.ShapeDtypeStruct((m, D_MODEL), F32),
        compiler_params=_cparams(("arbitrary",)),
        name=name,
    )(ya, yb, z, z, x, mod, final_g, wa16, wb16, wo16)


def _pad_lora(a, axis):
    shp = a.shape
    a = a.reshape(shp[:axis] + (2 * N_DIR, R_LORA) + shp[axis + 1:])
    pad = [(0, 0)] * a.ndim
    pad[axis + 1] = (0, R_PAD - R_LORA)
    a = jnp.pad(a, pad)
    return a.reshape(shp[:axis] + (2 * N_DIR * R_PAD,) + shp[axis + 1:])


def _stream(x, zs, z, yb, row0, seq_len, mod, mod_row0, mod_row_step, lw, s0, emit_state, tag):
    (p8, w0a0, w2a2, wa16, wb16, wo16, final_g) = lw
    m = x.shape[0]
    wkv_out = _wkv_call(zs, z, row0, m, p8, w0a0, w2a2, seq_len, s0=s0, emit_state=emit_state, name="wkv_" + tag)
    ya = wkv_out[0]
    y = _outproj_call(ya, yb, z, row0, x, mod, final_g, wa16, wb16, wo16, seq_len // OUT_TM,
                      mod_row0, mod_row_step, name="outproj_" + tag)
    return y, wkv_out[1:]


def kernel(x_prompt, x_sample, state_wkv_fwd, state_wkv_bwd, c, c_ctx, ada_w, ada_b, norm_g, w_in, mu_prev,
           mu_next, w0, w2, a0, a2, k_k, k_a, r_k, lnx_g, lnx_b, conv_w, w_out_a, w_out_b, w_o, final_g):
    depth = norm_g.shape[0]
    assert depth == 1
    l = 0
    nbp, seq_p, _ = x_prompt.shape
    nbs, seq_s, _ = x_sample.shape
    lat_rows = seq_s // GRID_W
    xc = x_prompt.reshape(nbp * seq_p, D_MODEL)
    xl = x_sample.reshape(nbs * seq_s, D_MODEL)

    cvec = jnp.concatenate([c, c_ctx[None, :], jnp.zeros((8 - nbs - 1, D_MODEL), F32)], axis=0)
    mod = _ada_call(cvec, ada_w[l], ada_b[l][None, :]).reshape(8, 1, 3 * D_MODEL)

    w_lora = _pad_lora(w_in[l][:, N_RKV:N_SHIFTED], 1)
    mu_p = jnp.concatenate([mu_prev[l][None, :N_RKV], _pad_lora(mu_prev[l][None, N_RKV:], 1)], axis=1)
    mu_n = jnp.concatenate([mu_next[l][None, :N_RKV], _pad_lora(mu_next[l][None, N_RKV:], 1)], axis=1)
    zrow = jnp.zeros((D_A,), F32)
    p8 = jnp.stack([k_k[l], k_a[l], r_k[l].reshape(D_A), lnx_g[l], lnx_b[l], zrow, zrow, zrow], axis=0)
    w0a0 = jnp.concatenate([w0[l], a0[l]], axis=0)
    w2a2 = jnp.pad(jnp.concatenate([w2[l], a2[l]], axis=0), ((0, 0), (0, R_PAD - R_LORA), (0, 0)))
    lw = (p8, w0a0, w2a2, w_out_a[l].astype(BF16), w_out_b[l].astype(BF16), w_o[l].astype(BF16), final_g[None, :])

    h = _norm_call(xc, xl, mod, norm_g[l][None, :], seq_s, nbs)
    zs = _inproj_shift_call(h, w_in[l], w_lora, mu_p, mu_n, xc.shape[0], seq_p, seq_s)
    z = _inproj_gates_call(h, w_in[l])
    yb = _inproj_conv_call(h, w_in[l], conv_w[l], xc.shape[0], seq_p, seq_s // lat_rows)

    y_prompt, st = _stream(xc, zs, z, yb, 0, seq_p, mod, nbs, 0, lw, None, True, "ctx")
    s0 = (state_wkv_fwd[:, l].reshape(nbs, D_A, HEAD), state_wkv_bwd[:, l].reshape(nbs, D_A, HEAD))
    y_sample, _ = _stream(xl, zs, z, yb, xc.shape[0], seq_s, mod, 0, 1, lw, s0, False, "lat")
    new_f = st[0].reshape(nbp, 1, N_HEADS, HEAD, HEAD)
    new_b = st[1].reshape(nbp, 1, N_HEADS, HEAD, HEAD)
    return (y_prompt.reshape(nbp, seq_p, D_MODEL), y_sample.reshape(nbs, seq_s, D_MODEL), new_f, new_b)
```

```python
import functools

import jax
import jax.numpy as jnp
from jax import lax
from jax.experimental import pallas as pl
from jax.experimental.pallas import tpu as pltpu

F32 = jnp.float32
BF16 = jnp.bfloat16

D_MODEL = 2048
D_A = 2048
D_B = 2048
HEAD = 64
N_HEADS = D_A // HEAD
R_LORA = 96
R_PAD = 128
N_DIR = 2
GRID_W = 64
RMS_EPS = 1e-6
GN_EPS = HEAD * 1e-5
N_RKV = 3 * D_A
N_LORA = 2 * N_DIR * R_LORA
N_SHIFTED = N_RKV + N_LORA
N_REST = D_A + 4 * D_B + 2 * D_MODEL

CHUNK = 64
INV_BASE = 4
G_HEADS = 4
GC = G_HEADS * HEAD
N_GROUPS = N_HEADS // G_HEADS
WKV_ROWS = 256

V7X_VMEM_LIMIT = 60 * 1024 * 1024


def _cparams(sem):
    return pltpu.CompilerParams(dimension_semantics=sem, vmem_limit_bytes=V7X_VMEM_LIMIT)


def _bdot(a, b):
    return jnp.dot(a.astype(BF16), b.astype(BF16), preferred_element_type=F32)


def _bdot_nt(a, b):
    return lax.dot_general(a.astype(BF16), b.astype(BF16), (((1,), (1,)), ((), ())),
                           preferred_element_type=F32)


def _bdot_tn(a, b):
    return lax.dot_general(a.astype(BF16), b.astype(BF16), (((0,), (0,)), ((), ())),
                           preferred_element_type=F32)


def _split3(x):
    h1 = x.astype(BF16)
    r1 = x - h1.astype(F32)
    h2 = r1.astype(BF16)
    h3 = (r1 - h2.astype(F32)).astype(BF16)
    return h1, h2, h3


def _dot_exact_rhs(x, m16):
    h1, h2, h3 = _split3(x)
    d = lambda a: jnp.dot(a, m16, preferred_element_type=F32)
    return d(h1) + d(h2) + d(h3)


def _sigmoid(x):
    return 1.0 / (1.0 + jnp.exp(-x))


EXP_M05 = 0.6065306597126334


def _ada_kernel(c_ref, w_ref, b_ref, o_ref):
    c = c_ref[...]
    s = c * _sigmoid(c)
    o_ref[...] = jnp.dot(s, w_ref[...], precision=lax.Precision.HIGHEST,
                         preferred_element_type=F32) + b_ref[...]


def _ada_call(cvec, ada_w, ada_b):
    n = ada_w.shape[1]
    tn = 1024
    return pl.pallas_call(
        _ada_kernel,
        grid=(n // tn,),
        in_specs=[pl.BlockSpec((8, D_MODEL), lambda j: (0, 0)),
                  pl.BlockSpec((D_MODEL, tn), lambda j: (0, j)),
                  pl.BlockSpec((1, tn), lambda j: (0, j))],
        out_specs=pl.BlockSpec((8, tn), lambda j: (0, j)),
        out_shape=jax.ShapeDtypeStruct((8, n), F32),
        compiler_params=_cparams(("arbitrary",)),
        name="ada",
    )(cvec, ada_w, ada_b)


NORM_TM = 512
NORM_ROWS = 128


def _norm_kernel(xc_ref, xl_ref, sh_ref, sc_ref, g_ref, o_ref, *, n_ctx_tiles):
    g = g_ref[...]
    sc = 1.0 + sc_ref[...]
    sh = sh_ref[...]

    def run(x_ref):
        def body(i, carry):
            rows = pl.ds(pl.multiple_of(i * NORM_ROWS, NORM_ROWS), NORM_ROWS)
            x = x_ref[rows, :]
            ms = jnp.mean(x * x, axis=-1, keepdims=True)
            xn = x * lax.rsqrt(ms + RMS_EPS) * g
            o_ref[rows, :] = (xn * sc + sh).astype(BF16)
            return carry

        lax.fori_loop(0, NORM_TM // NORM_ROWS, body, 0)

    @pl.when(pl.program_id(0) < n_ctx_tiles)
    def _():
        run(xc_ref)

    @pl.when(pl.program_id(0) >= n_ctx_tiles)
    def _():
        run(xl_ref)


def _norm_call(xc, xl, mod, norm_g, seq_lat, ctx_row):
    mc, ml = xc.shape[0], xl.shape[0]
    assert mc % NORM_TM == 0 and ml % NORM_TM == 0 and seq_lat % NORM_TM == 0
    ntc = mc // NORM_TM
    tiles_per_seq = seq_lat // NORM_TM
    row = lambda i: jnp.where(i < ntc, ctx_row, (i - ntc) // tiles_per_seq)
    return pl.pallas_call(
        functools.partial(_norm_kernel, n_ctx_tiles=ntc),
        grid=((mc + ml) // NORM_TM,),
        in_specs=[pl.BlockSpec((NORM_TM, D_MODEL), lambda i: (jnp.minimum(i, ntc - 1), 0)),
                  pl.BlockSpec((NORM_TM, D_MODEL), lambda i: (jnp.maximum(i - ntc, 0), 0)),
                  pl.BlockSpec((None, 1, D_MODEL), lambda i: (row(i), 0, 0)),
                  pl.BlockSpec((None, 1, D_MODEL), lambda i: (row(i), 0, 1)),
                  pl.BlockSpec((1, D_MODEL), lambda i: (0, 0))],
        out_specs=pl.BlockSpec((NORM_TM, D_MODEL), lambda i: (i, 0)),
        out_shape=jax.ShapeDtypeStruct((mc + ml, D_MODEL), BF16),
        compiler_params=_cparams(("arbitrary",)),
        name="norm",
    )(xc, xl, mod, mod, norm_g)


IN_TM = 1024
IN_TN = 512
LORA_TILE = N_RKV // IN_TN
Z_LORA_COL = N_RKV
ZS_COLS = N_RKV + IN_TN


REST_TM = 2048


def _inproj_rest_kernel(h_ref, w_ref, o_ref):
    rows = pl.ds(pl.multiple_of(pl.program_id(1) * REST_TM, REST_TM), REST_TM)
    o_ref[...] = jnp.dot(h_ref[rows, :], w_ref[...].astype(BF16), preferred_element_type=F32)


COL_GA = N_SHIFTED
COL_CONV = COL_GA + D_A
COL_MERGE = COL_CONV + 4 * D_B
N_GATES = D_A + 2 * D_MODEL


def _inproj_gates_call(h, w_in):
    m = h.shape[0]
    assert m % REST_TM == 0 and D_A % IN_TN == 0 and COL_GA % 128 == 0
    n_ga = D_A // IN_TN
    w_col = lambda j: pl.multiple_of(jnp.where(j < n_ga, COL_GA + j * IN_TN, COL_MERGE + (j - n_ga) * IN_TN), 128)
    return pl.pallas_call(
        _inproj_rest_kernel,
        grid=(N_GATES // IN_TN, m // REST_TM),
        in_specs=[pl.BlockSpec((m, D_MODEL), lambda j, i: (0, 0), pipeline_mode=pl.Buffered(1)),
                  pl.BlockSpec((pl.Element(D_MODEL), pl.Element(IN_TN)), lambda j, i: (0, w_col(j)))],
        out_specs=pl.BlockSpec((REST_TM, IN_TN), lambda j, i: (i, j)),
        out_shape=jax.ShapeDtypeStruct((m, N_GATES), F32),
        compiler_params=_cparams(("arbitrary", "arbitrary")),
        name="inproj_gates",
    )(h, w_in)


CONV_TN = 256


def _inproj_conv_kernel(h_ref, wb_ref, wc_ref, wx_ref, wg_ref, cw_ref, o_ref, *, n_ctx_tiles, row_ctx, row_lat):
    i = pl.program_id(1)
    rows = pl.ds(pl.multiple_of(i * IN_TM, IN_TM), IN_TM)
    proj = lambda w_ref: jnp.dot(h_ref[rows, :], w_ref[...].astype(BF16), preferred_element_type=F32)
    p = proj(wc_ref) * proj(wx_ref)
    row_len = jnp.where(i < n_ctx_tiles, row_ctx, row_lat)
    pos = lax.broadcasted_iota(jnp.int32, p.shape, 0) & (row_len - 1)
    pp = jnp.where(pos == 0, 0.0, pltpu.roll(p, 1, 0))
    pn = jnp.where(pos == row_len - 1, 0.0, pltpu.roll(p, IN_TM - 1, 0))
    w = cw_ref[...]
    u = w[0:1] * pp + w[1:2] * p + w[2:3] * pn
    g = proj(wg_ref)
    o_ref[...] = (proj(wb_ref) * u * (g * _sigmoid(g))).astype(BF16)


def _inproj_conv_call(h, w_in, conv_w, m_ctx, row_ctx, row_lat):
    m = h.shape[0]
    assert m % IN_TM == 0 and m_ctx % IN_TM == 0 and IN_TM % row_ctx == 0 and IN_TM % row_lat == 0
    wspec = lambda q: pl.BlockSpec((pl.Element(D_MODEL), pl.Element(CONV_TN)),
                                   lambda j, i: (0, pl.multiple_of(COL_CONV + q * D_B + j * CONV_TN, 128)))
    return pl.pallas_call(
        functools.partial(_inproj_conv_kernel, n_ctx_tiles=m_ctx // IN_TM, row_ctx=row_ctx, row_lat=row_lat),
        grid=(D_B // CONV_TN, m // IN_TM),
        in_specs=[pl.BlockSpec((m, D_MODEL), lambda j, i: (0, 0), pipeline_mode=pl.Buffered(1)),
                  wspec(0), wspec(1), wspec(2), wspec(3),
                  pl.BlockSpec((3, CONV_TN), lambda j, i: (0, j))],
        out_specs=pl.BlockSpec((IN_TM, CONV_TN), lambda j, i: (i, j)),
        out_shape=jax.ShapeDtypeStruct((m, D_B), BF16),
        compiler_params=_cparams(("arbitrary", "arbitrary")),
        name="inproj_conv",
    )(h, w_in, w_in, w_in, w_in, conv_w)


def _inproj_shift_kernel(h_ref, w_ref, wl_ref, mup_ref, mun_ref, o_ref, *, n_ctx_tiles, seq_ctx, seq_lat):
    j = pl.program_id(0)
    i = pl.program_id(1)
    rows = pl.ds(pl.multiple_of(i * IN_TM, IN_TM), IN_TM)
    w = jnp.where(j == LORA_TILE, wl_ref[...], w_ref[...]).astype(BF16)
    z = jnp.dot(h_ref[rows, :], w, preferred_element_type=F32)
    mup = mup_ref[...]
    mun = mun_ref[...]
    o_ref[...] = z * (1.0 - mup - mun) + pltpu.roll(z, 1, 0) * mup + pltpu.roll(z, IN_TM - 1, 0) * mun

    def unwrap(seq_len):
        for first in range(0, IN_TM, seq_len):
            last = first + seq_len - 1
            before = (first - 1) % IN_TM
            after = (last + 1) % IN_TM
            o_ref[first:first + 1, :] = o_ref[first:first + 1, :] - mup * z[before:before + 1]
            o_ref[last:last + 1, :] = o_ref[last:last + 1, :] - mun * z[after:after + 1]

    @pl.when(i < n_ctx_tiles)
    def _():
        unwrap(seq_ctx)

    @pl.when(i >= n_ctx_tiles)
    def _():
        unwrap(seq_lat)


def _inproj_shift_call(h, w_in, w_lora, mu_prev, mu_next, m_ctx, seq_ctx, seq_lat):
    m = h.shape[0]
    assert m % IN_TM == 0 and m_ctx % IN_TM == 0 and IN_TM % seq_ctx == 0 and IN_TM % seq_lat == 0
    w_col = lambda j: pl.multiple_of(jnp.minimum(j, LORA_TILE - 1) * IN_TN, 128)
    resident = dict(pipeline_mode=pl.Buffered(1))
    return pl.pallas_call(
        functools.partial(_inproj_shift_kernel, n_ctx_tiles=m_ctx // IN_TM, seq_ctx=seq_ctx, seq_lat=seq_lat),
        grid=(LORA_TILE + 1, m // IN_TM),
        in_specs=[pl.BlockSpec((m, D_MODEL), lambda j, i: (0, 0), **resident),
                  pl.BlockSpec((pl.Element(D_MODEL), pl.Element(IN_TN)), lambda j, i: (0, w_col(j))),
                  pl.BlockSpec((D_MODEL, IN_TN), lambda j, i: (0, 0), **resident),
                  pl.BlockSpec((1, IN_TN), lambda j, i: (0, j)),
                  pl.BlockSpec((1, IN_TN), lambda j, i: (0, j))],
        out_specs=pl.BlockSpec((IN_TM, IN_TN), lambda j, i: (i, j)),
        out_shape=jax.ShapeDtypeStruct((m, ZS_COLS), F32),
        compiler_params=_cparams(("arbitrary", "arbitrary")),
        name="inproj_shift",
    )(h, w_in, w_lora, mu_prev, mu_next)


WKV_BLK = 1024
N_STAGE = 4
STAGE_CHUNKS = WKV_BLK // CHUNK // N_STAGE


def _lockstep(tasks):
    tasks = list(tasks)
    while tasks:
        alive = []
        for t in tasks:
            try:
                next(t)
                alive.append(t)
            except StopIteration:
                pass
        tasks = alive


def _wkv_kernel(*refs, seq_len, has_s0, emit_state):
    it = iter(refs)
    r_ref, k_ref, v_ref, lora_ref, ga_ref, p8_ref, w0a0_ref, w2a2_ref = (next(it) for _ in range(8))
    s0_refs = (next(it), next(it)) if has_s0 else None
    ya_ref = next(it)
    st_refs = (next(it), next(it)) if emit_state else None
    (atrt_ref, arb_ref, bkt_ref, u0_ref, y0_ref, pct_ref, kk_ref, lw_ref, kd_ref, bb_ref,
     yacc_ref, bonus_ref, s_ref) = (next(it) for _ in range(13))

    n_seq = WKV_BLK // seq_len
    n_chunks = WKV_BLK // CHUNK
    p8 = p8_ref[...]
    k_k, k_a, r_k, lnx_g, lnx_b = (p8[i:i + 1] for i in range(5))
    w0a0 = w0a0_ref[...]

    row_c = lax.broadcasted_iota(jnp.int32, (CHUNK, GC), 0)
    col_c = lax.broadcasted_iota(jnp.int32, (CHUNK, GC), 1) & (HEAD - 1)
    strict = (col_c < row_c, col_c > row_c)
    incl = (col_c <= row_c, col_c >= row_c)
    eye_pk = jnp.where(col_c == row_c, 1.0, 0.0)
    same_blk = {}
    m = INV_BASE
    while m <= CHUNK:
        same_blk[m] = (row_c // m) == (col_c // m)
        m *= 2
    near = [{k: same_blk[INV_BASE] & (sg * (row_c - col_c) == k) for k in (2, 3)} for sg in (1, -1)]
    rb = lax.broadcasted_iota(jnp.int32, (GC, GC), 0) // HEAD
    cb = lax.broadcasted_iota(jnp.int32, (GC, GC), 1) // HEAD
    bd_mask = rb == cb
    ones_bd16 = jnp.where(bd_mask, 1.0, 0.0).astype(BF16)

    def bd16(x):
        t = jnp.concatenate([x.astype(F32)] * G_HEADS, axis=0)
        return jnp.where(bd_mask, t, 0.0).astype(BF16)

    def split2(x):
        h1 = x.astype(BF16)
        return h1, (x - h1.astype(F32)).astype(BF16)

    def ones_dot(x16):
        return jnp.dot(x16, ones_bd16, preferred_element_type=F32)

    def running_sum(x, d):
        s = 1
        while s < CHUNK:
            if d == 0:
                x = x + jnp.where(row_c >= s, pltpu.roll(x, s, 0), 0.0)
            else:
                x = x + jnp.where(row_c < CHUNK - s, pltpu.roll(x, CHUNK - s, 0), 0.0)
            s *= 2
        return x

    def prep(j):
        rows = pl.ds(j * WKV_ROWS, WKV_ROWS)
        r = r_ref[rows, :]
        k = k_ref[rows, :]
        lo = lora_ref[rows, :]
        kk = k * k_k
        q1, q2 = split2(kk * kk)
        ssq = ones_dot(q1)
        yield
        ssq = ssq + ones_dot(q2)
        yield
        kk = kk / jnp.maximum(jnp.sqrt(ssq), 1e-12)
        kk_ref[rows, :] = kk
        kd_sum = None
        for d in range(N_DIR):
            wd = jnp.tanh(lo[:, d * R_PAD:(d + 1) * R_PAD])
            ad = lo[:, (N_DIR + d) * R_PAD:(N_DIR + d + 1) * R_PAD]
            w_lin = w0a0[d:d + 1] + _bdot(wd, w2a2_ref[d])
            yield
            lw_ref[d, rows, :] = -EXP_M05 * _sigmoid(w_lin)
            a = _sigmoid(w0a0[N_DIR + d:N_DIR + d + 1] + _bdot(ad, w2a2_ref[N_DIR + d]))
            yield
            kd = k * (1.0 + (a - 1.0) * k_a)
            kd_ref[d, rows, :] = kd
            bb_ref[d, rows, :] = kk * a
            kd_sum = kd if kd_sum is None else kd_sum + kd
        b1, b2 = split2(r * kd_sum * r_k)
        bs = ones_dot(b1)
        yield
        bs = bs + ones_dot(b2)
        yield
        bonus_ref[rows, :] = bs * v_ref[rows, :]

    def a_job(c, d):
        rows = pl.ds(c * CHUNK, CHUNK)
        r = r_ref[rows, :]
        kk = kk_ref[rows, :]
        lw = lw_ref[d, rows, :]
        cs = running_sum(lw, d)
        e = jnp.exp(cs)
        e_inv = jnp.exp(-cs)
        rt = r * e
        at = -kk * jnp.exp(cs - lw)
        bt = bb_ref[d, rows, :] * e_inv
        kt = kd_ref[d, rows, :] * e_inv
        atrt_ref[d, c, CHUNK:2 * CHUNK, :] = rt.astype(BF16)
        bkt_ref[d, c] = jnp.concatenate([bt, kt], axis=0).T.astype(BF16)
        edge = e[CHUNK - 1:CHUNK] if d == 0 else e[0:1]
        pct_ref[d, c] = jnp.broadcast_to(edge, (GC // 2, GC)).T
        ar = jnp.concatenate([at, rt], axis=0).astype(BF16)
        sb = _bdot_nt(ar, bd16(bt))
        yield
        sk = _bdot_nt(ar, bd16(kt))
        yield
        l_ab = jnp.where(strict[d], sb[:CHUNK], 0.0)
        arb_ref[d, c] = jnp.where(incl[d], sb[CHUNK:], 0.0).astype(BF16)
        lak_ark = jnp.concatenate([jnp.where(strict[d], sk[:CHUNK], 0.0),
                                   jnp.where(incl[d], sk[CHUNK:], 0.0)], axis=0).astype(BF16)
        sgn = 1 if d == 0 else -1
        l4 = jnp.where(same_blk[INV_BASE], l_ab, 0.0)
        a1 = pltpu.roll(l4, (-sgn) % GC, 1)
        a2 = pltpu.roll(l4, (-2 * sgn) % GC, 1)
        t2 = l4 + jnp.where(near[d][2], a1 * pltpu.roll(l4, sgn % CHUNK, 0), 0.0)
        t3 = a1 * pltpu.roll(l4, (2 * sgn) % CHUNK, 0) + a2 * pltpu.roll(t2, sgn % CHUNK, 0)
        tm = eye_pk + t2 + jnp.where(near[d][3], t3, 0.0)
        m = INV_BASE
        while m < CHUNK:
            off = same_blk[2 * m] & jnp.logical_not(same_blk[m])
            m1 = _bdot(tm, bd16(jnp.where(off, l_ab, 0.0)))
            yield
            tm = tm + _bdot(m1, bd16(tm))
            yield
            m *= 2
        atrt_ref[d, c, 0:CHUNK, :] = _bdot(tm, bd16(at)).astype(BF16)
        yield
        xv = _bdot(lak_ark, bd16(v_ref[rows, :]))
        yield
        y0_ref[d, c] = xv[CHUNK:]
        u0_ref[d, c] = _bdot(tm, bd16(xv[:CHUNK]))
        yield

    def b_chain(d, chunks, init, out_seq):
        if init == "zero":
            s_ref[d] = jnp.zeros((GC, GC), F32)
        elif init == "given":
            tile16 = jnp.where(lax.broadcasted_iota(jnp.int32, (HEAD, GC), 0)
                               == (lax.broadcasted_iota(jnp.int32, (HEAD, GC), 1) & (HEAD - 1)),
                               1.0, 0.0).astype(BF16)
            s_ref[d] = jnp.where(bd_mask, _dot_exact_rhs(s0_refs[d][0], tile16), 0.0).T
            yield
        for c in chunks:
            rows = pl.ds(c * CHUNK, CHUNK)
            s = s_ref[d]
            us = _bdot(atrt_ref[d, c], s.astype(BF16))
            yield
            u = us[:CHUNK] + u0_ref[d, c]
            uv = jnp.concatenate([u.astype(BF16), v_ref[rows, :].astype(BF16)], axis=0)
            ds = _bdot(bkt_ref[d, c], uv)
            yield
            pcc = pct_ref[d, c]
            s_ref[d] = (s + jnp.where(bd_mask, ds, 0.0)) * jnp.concatenate([pcc, pcc], axis=1)
            y = us[CHUNK:] + _bdot(arb_ref[d, c], bd16(u)) + y0_ref[d, c]
            yield
            yacc_ref[rows, :] += y
        if out_seq is not None:
            untile16 = jnp.where((lax.broadcasted_iota(jnp.int32, (GC, HEAD), 0) & (HEAD - 1))
                                 == lax.broadcasted_iota(jnp.int32, (GC, HEAD), 1), 1.0, 0.0).astype(BF16)
            st_refs[d][out_seq] = _dot_exact_rhs(s_ref[d].T, untile16)
            yield

    def post(j):
        rows = pl.ds(j * WKV_ROWS, WKV_ROWS)
        y = yacc_ref[rows, :]
        y1, y2 = split2(y)
        mu = ones_dot(y1)
        yield
        mu = (mu + ones_dot(y2)) * (1.0 / HEAD)
        yield
        yc = y - mu
        v1, v2 = split2(yc * yc)
        var = ones_dot(v1)
        yield
        var = (var + ones_dot(v2)) * (1.0 / HEAD)
        yield
        out = yc * lax.rsqrt(var + GN_EPS) * lnx_g + lnx_b + bonus_ref[rows, :]
        ga = ga_ref[rows, :]
        ya_ref[rows, :] = (out * (ga * _sigmoid(ga))).astype(BF16)

    yacc_ref[...] = jnp.zeros_like(yacc_ref)
    sc = STAGE_CHUNKS
    if n_seq == 1:
        fwd = [[i * sc + q for q in range(sc)] for i in range(N_STAGE)]
        bwd = [[n_chunks - 1 - i * sc - q for q in range(sc)] for i in range(N_STAGE)]
        a_jobs = [[a_job(c, 0) for c in fwd[i]] + [a_job(c, 1) for c in bwd[i]] for i in range(N_STAGE)]
        init = lambda i: ("given" if has_s0 else "zero") if i == 0 else None
        chains = [[b_chain(0, fwd[i], init(i), None), b_chain(1, bwd[i], init(i), None)] for i in range(N_STAGE)]
        stages = [
            ([], [prep(0), prep(N_STAGE - 1)]),
            (a_jobs[0], [prep(j) for j in range(1, N_STAGE - 1)]),
        ]
        for i in range(1, N_STAGE):
            stages.append((a_jobs[i], chains[i - 1]))
        stages.append(([], chains[N_STAGE - 1] + [post(j) for j in range(1, N_STAGE - 1)]))
        stages.append(([], [post(0), post(N_STAGE - 1)]))
    else:
        assert n_seq == N_STAGE and seq_len == sc * CHUNK
        fwd = [[i * sc + q for q in range(sc)] for i in range(N_STAGE)]
        bwd = [list(reversed(f)) for f in fwd]
        a_jobs = [[a_job(c, 0) for c in fwd[i]] + [a_job(c, 1) for c in bwd[i]] for i in range(N_STAGE)]
        out_seq = (lambda i: i) if emit_state else (lambda i: None)
        chains = [[b_chain(0, fwd[i], "zero", out_seq(i)), b_chain(1, bwd[i], "zero", out_seq(i))]
                  for i in range(N_STAGE)]
        stages = [([], [prep(0)]), (a_jobs[0], [prep(1)])]
        for i in range(1, N_STAGE):
            side = list(chains[i - 1])
            if i + 1 < N_STAGE:
                side.append(prep(i + 1))
            if i >= 2:
                side.append(post(i - 2))
            stages.append((a_jobs[i], side))
        stages.append(([], chains[N_STAGE - 1] + [post(N_STAGE - 2)]))
        stages.append(([], [post(N_STAGE - 1)]))
    for jobs, side in stages:
        _lockstep(jobs + side)


def _wkv_call(zs, z, row0, m, p8, w0a0, w2a2, seq_len, s0=None, emit_state=False, name="wkv"):
    assert row0 % WKV_BLK == 0 and m % WKV_BLK == 0 and WKV_BLK % seq_len == 0
    n_seq = WKV_BLK // seq_len
    nb = m // WKV_BLK
    b0 = row0 // WKV_BLK
    nc = WKV_BLK // CHUNK
    has_s0 = s0 is not None
    assert not has_s0 or n_seq == 1
    ncb = D_A // GC
    lora_w = 2 * N_DIR * R_PAD
    in_specs = [
        pl.BlockSpec((WKV_BLK, GC), lambda b, g: (b0 + b, g)),
        pl.BlockSpec((WKV_BLK, GC), lambda b, g: (b0 + b, ncb + g)),
        pl.BlockSpec((WKV_BLK, GC), lambda b, g: (b0 + b, 2 * ncb + g)),
        pl.BlockSpec((WKV_BLK, lora_w), lambda b, g: (b0 + b, Z_LORA_COL // lora_w)),
        pl.BlockSpec((WKV_BLK, GC), lambda b, g: (b0 + b, g)),
        pl.BlockSpec((8, GC), lambda b, g: (0, g)),
        pl.BlockSpec((2 * N_DIR, GC), lambda b, g: (0, g)),
        pl.BlockSpec((2 * N_DIR, R_PAD, GC), lambda b, g: (0, 0, g)),
    ]
    args = [zs, zs, zs, zs, z, p8, w0a0, w2a2]
    if has_s0:
        in_specs += [pl.BlockSpec((1, GC, HEAD), lambda b, g: (b, g, 0))] * 2
        args += list(s0)
    out_specs = [pl.BlockSpec((WKV_BLK, GC), lambda b, g: (b, g))]
    out_shape = [jax.ShapeDtypeStruct((m, D_A), BF16)]
    if emit_state:
        out_specs += [pl.BlockSpec((n_seq, GC, HEAD), lambda b, g: (b, g, 0))] * 2
        out_shape += [jax.ShapeDtypeStruct((nb * n_seq, D_A, HEAD), F32)] * 2
    pair16 = pltpu.VMEM((N_DIR, nc, 2 * CHUNK, GC), BF16)
    big16 = pltpu.VMEM((N_DIR, nc, CHUNK, GC), BF16)
    big32 = pltpu.VMEM((N_DIR, nc, CHUNK, GC), F32)
    seq32 = pltpu.VMEM((WKV_BLK, GC), F32)
    dir32 = pltpu.VMEM((N_DIR, WKV_BLK, GC), F32)
    scratch = [pair16, big16, pltpu.VMEM((N_DIR, nc, GC, 2 * CHUNK), BF16), big32, big32,
               pltpu.VMEM((N_DIR, nc, GC, GC // 2), F32),
               seq32, dir32, dir32, dir32,
               seq32, seq32,
               pltpu.VMEM((N_DIR, GC, GC), F32)]
    return pl.pallas_call(
        functools.partial(_wkv_kernel, seq_len=seq_len, has_s0=has_s0, emit_state=emit_state),
        grid=(nb, N_GROUPS),
        in_specs=in_specs,
        out_specs=out_specs,
        out_shape=out_shape,
        scratch_shapes=scratch,
        compiler_params=_cparams(("arbitrary", "arbitrary")),
        name=name,
    )(*args)


OUT_TM = 256


def _outproj_kernel(ya_ref, yb_ref, ma_ref, mb_ref, x_ref, gate_ref, fg_ref, wa_ref, wb_ref, wo_ref, o_ref):
    y_a = jnp.dot(ya_ref[...], wa_ref[...], preferred_element_type=F32)
    y_b = jnp.dot(yb_ref[...], wb_ref[...], preferred_element_type=F32)
    merged = _sigmoid(ma_ref[...]) * y_a + _sigmoid(mb_ref[...]) * y_b
    out = jnp.dot(merged.astype(BF16), wo_ref[...], preferred_element_type=F32)
    xo = x_ref[...] + gate_ref[...] * out
    ms = jnp.mean(xo * xo, axis=-1, keepdims=True)
    o_ref[...] = xo * lax.rsqrt(ms + RMS_EPS) * fg_ref[...]


def _outproj_call(ya, yb, z, row0, x, mod, final_g, wa16, wb16, wo16, rows_per_mod, mod_row0, mod_row_step,
                  name="outproj"):
    m = x.shape[0]
    mcol = D_A // D_MODEL
    i0 = row0 // OUT_TM
    row = lambda i: mod_row0 + mod_row_step * (i // rows_per_mod)
    full = lambda: pl.BlockSpec((D_MODEL, D_MODEL), lambda i: (0, 0), pipeline_mode=pl.Buffered(1))
    tile = lambda col: pl.BlockSpec((OUT_TM, D_MODEL), lambda i: (i, col))
    ztile = lambda col: pl.BlockSpec((OUT_TM, D_MODEL), lambda i: (i0 + i, col))
    return pl.pallas_call(
        _outproj_kernel,
        grid=(m // OUT_TM,),
        in_specs=[tile(0), ztile(0), ztile(mcol), ztile(mcol + 1), tile(0),
                  pl.BlockSpec((None, 1, D_MODEL), lambda i: (row(i), 0, 2)),
                  pl.BlockSpec((1, D_MODEL), lambda i: (0, 0)),
                  full(), full(), full()],
        out_specs=tile(0),
        out_shape=jax.ShapeDtypeStruct((m, D_MODEL), F32),
        compiler_params=_cparams(("arbitrary",)),
        name=name,
    )(ya, yb, z, z, x, mod, final_g, wa16, wb16, wo16)


def _pad_lora(a, axis):
    shp = a.shape
    a = a.reshape(shp[:axis] + (2 * N_DIR, R_LORA) + shp[axis + 1:])
    pad = [(0, 0)] * a.ndim
    pad[axis + 1] = (0, R_PAD - R_LORA)
    a = jnp.pad(a, pad)
    return a.reshape(shp[:axis] + (2 * N_DIR * R_PAD,) + shp[axis + 1:])


def _stream(x, zs, z, yb, row0, seq_len, mod, mod_row0, mod_row_step, lw, s0, emit_state, tag):
    (p8, w0a0, w2a2, wa16, wb16, wo16, final_g) = lw
    m = x.shape[0]
    wkv_out = _wkv_call(zs, z, row0, m, p8, w0a0, w2a2, seq_len, s0=s0, emit_state=emit_state, name="wkv_" + tag)
    ya = wkv_out[0]
    y = _outproj_call(ya, yb, z, row0, x, mod, final_g, wa16, wb16, wo16, seq_len // OUT_TM,
                      mod_row0, mod_row_step, name="outproj_" + tag)
    return y, wkv_out[1:]


def kernel(x_prompt, x_sample, state_wkv_fwd, state_wkv_bwd, c, c_ctx, ada_w, ada_b, norm_g, w_in, mu_prev,
           mu_next, w0, w2, a0, a2, k_k, k_a, r_k, lnx_g, lnx_b, conv_w, w_out_a, w_out_b, w_o, final_g):
    depth = norm_g.shape[0]
    assert depth == 1
    l = 0
    nbp, seq_p, _ = x_prompt.shape
    nbs, seq_s, _ = x_sample.shape
    lat_rows = seq_s // GRID_W
    xc = x_prompt.reshape(nbp * seq_p, D_MODEL)
    xl = x_sample.reshape(nbs * seq_s, D_MODEL)

    cvec = jnp.concatenate([c, c_ctx[None, :], jnp.zeros((8 - nbs - 1, D_MODEL), F32)], axis=0)
    mod = _ada_call(cvec, ada_w[l], ada_b[l][None, :]).reshape(8, 1, 3 * D_MODEL)

    w_lora = _pad_lora(w_in[l][:, N_RKV:N_SHIFTED], 1)
    mu_p = jnp.concatenate([mu_prev[l][None, :N_RKV], _pad_lora(mu_prev[l][None, N_RKV:], 1)], axis=1)
    mu_n = jnp.concatenate([mu_next[l][None, :N_RKV], _pad_lora(mu_next[l][None, N_RKV:], 1)], axis=1)
    zrow = jnp.zeros((D_A,), F32)
    p8 = jnp.stack([k_k[l], k_a[l], r_k[l].reshape(D_A), lnx_g[l], lnx_b[l], zrow, zrow, zrow], axis=0)
    w0a0 = jnp.concatenate([w0[l], a0[l]], axis=0)
    w2a2 = jnp.pad(jnp.concatenate([w2[l], a2[l]], axis=0), ((0, 0), (0, R_PAD - R_LORA), (0, 0)))
    lw = (p8, w0a0, w2a2, w_out_a[l].astype(BF16), w_out_b[l].astype(BF16), w_o[l].astype(BF16), final_g[None, :])

    h = _norm_call(xc, xl, mod, norm_g[l][None, :], seq_s, nbs)
    zs = _inproj_shift_call(h, w_in[l], w_lora, mu_p, mu_n, xc.shape[0], seq_p, seq_s)
    z = _inproj_gates_call(h, w_in[l])
    yb = _inproj_conv_call(h, w_in[l], conv_w[l], xc.shape[0], seq_p, seq_s // lat_rows)

    y_prompt, st = _stream(xc, zs, z, yb, 0, seq_p, mod, nbs, 0, lw, None, True, "ctx")
    s0 = (state_wkv_fwd[:, l].reshape(nbs, D_A, HEAD), state_wkv_bwd[:, l].reshape(nbs, D_A, HEAD))
    y_sample, _ = _stream(xl, zs, z, yb, xc.shape[0], seq_s, mod, 0, 1, lw, s0, False, "lat")
    new_f = st[0].reshape(nbp, 1, N_HEADS, HEAD, HEAD)
    new_b = st[1].reshape(nbp, 1, N_HEADS, HEAD, HEAD)
    return (y_prompt.reshape(nbp, seq_p, D_MODEL), y_sample.reshape(nbs, seq_s, D_MODEL), new_f, new_b)
```

```python
import functools

import jax
import jax.numpy as jnp
from jax import lax
from jax.experimental import pallas as pl
from jax.experimental.pallas import tpu as pltpu

F32 = jnp.float32
BF16 = jnp.bfloat16

D_MODEL = 2048
D_A = 2048
D_B = 2048
HEAD = 64
N_HEADS = D_A // HEAD
R_LORA = 96
R_PAD = 128
N_DIR = 2
GRID_W = 64
RMS_EPS = 1e-6
GN_EPS = HEAD * 1e-5
N_RKV = 3 * D_A
N_LORA = 2 * N_DIR * R_LORA
N_SHIFTED = N_RKV + N_LORA
N_REST = D_A + 4 * D_B + 2 * D_MODEL

CHUNK = 64
INV_BASE = 4
G_HEADS = 4
GC = G_HEADS * HEAD
N_GROUPS = N_HEADS // G_HEADS
WKV_ROWS = 256

V7X_VMEM_LIMIT = 60 * 1024 * 1024


def _cparams(sem):
    return pltpu.CompilerParams(dimension_semantics=sem, vmem_limit_bytes=V7X_VMEM_LIMIT)


def _bdot(a, b):
    return jnp.dot(a.astype(BF16), b.astype(BF16), preferred_element_type=F32)


def _bdot_nt(a, b):
    return lax.dot_general(a.astype(BF16), b.astype(BF16), (((1,), (1,)), ((), ())),
                           preferred_element_type=F32)


def _bdot_tn(a, b):
    return lax.dot_general(a.astype(BF16), b.astype(BF16), (((0,), (0,)), ((), ())),
                           preferred_element_type=F32)


def _split3(x):
    h1 = x.astype(BF16)
    r1 = x - h1.astype(F32)
    h2 = r1.astype(BF16)
    h3 = (r1 - h2.astype(F32)).astype(BF16)
    return h1, h2, h3


def _dot_exact_rhs(x, m16):
    h1, h2, h3 = _split3(x)
    d = lambda a: jnp.dot(a, m16, preferred_element_type=F32)
    return d(h1) + d(h2) + d(h3)


def _sigmoid(x):
    return 1.0 / (1.0 + jnp.exp(-x))


EXP_M05 = 0.6065306597126334


def _ada_kernel(c_ref, w_ref, b_ref, o_ref):
    c = c_ref[...]
    s = c * _sigmoid(c)
    o_ref[...] = jnp.dot(s, w_ref[...], precision=lax.Precision.HIGHEST,
                         preferred_element_type=F32) + b_ref[...]


def _ada_call(cvec, ada_w, ada_b):
    n = ada_w.shape[1]
    tn = 1024
    return pl.pallas_call(
        _ada_kernel,
        grid=(n // tn,),
        in_specs=[pl.BlockSpec((8, D_MODEL), lambda j: (0, 0)),
                  pl.BlockSpec((D_MODEL, tn), lambda j: (0, j)),
                  pl.BlockSpec((1, tn), lambda j: (0, j))],
        out_specs=pl.BlockSpec((8, tn), lambda j: (0, j)),
        out_shape=jax.ShapeDtypeStruct((8, n), F32),
        compiler_params=_cparams(("arbitrary",)),
        name="ada",
    )(cvec, ada_w, ada_b)


NORM_TM = 512
NORM_ROWS = 128


def _norm_kernel(xc_ref, xl_ref, sh_ref, sc_ref, g_ref, o_ref, *, n_ctx_tiles):
    g = g_ref[...]
    sc = 1.0 + sc_ref[...]
    sh = sh_ref[...]

    def run(x_ref):
        def body(i, carry):
            rows = pl.ds(pl.multiple_of(i * NORM_ROWS, NORM_ROWS), NORM_ROWS)
            x = x_ref[rows, :]
            ms = jnp.mean(x * x, axis=-1, keepdims=True)
            xn = x * lax.rsqrt(ms + RMS_EPS) * g
            o_ref[rows, :] = (xn * sc + sh).astype(BF16)
            return carry

        lax.fori_loop(0, NORM_TM // NORM_ROWS, body, 0)

    @pl.when(pl.program_id(0) < n_ctx_tiles)
    def _():
        run(xc_ref)

    @pl.when(pl.program_id(0) >= n_ctx_tiles)
    def _():
        run(xl_ref)


def _norm_call(xc, xl, mod, norm_g, seq_lat, ctx_row):
    mc, ml = xc.shape[0], xl.shape[0]
    assert mc % NORM_TM == 0 and ml % NORM_TM == 0 and seq_lat % NORM_TM == 0
    ntc = mc // NORM_TM
    tiles_per_seq = seq_lat // NORM_TM
    row = lambda i: jnp.where(i < ntc, ctx_row, (i - ntc) // tiles_per_seq)
    return pl.pallas_call(
        functools.partial(_norm_kernel, n_ctx_tiles=ntc),
        grid=((mc + ml) // NORM_TM,),
        in_specs=[pl.BlockSpec((NORM_TM, D_MODEL), lambda i: (jnp.minimum(i, ntc - 1), 0)),
                  pl.BlockSpec((NORM_TM, D_MODEL), lambda i: (jnp.maximum(i - ntc, 0), 0)),
                  pl.BlockSpec((None, 1, D_MODEL), lambda i: (row(i), 0, 0)),
                  pl.BlockSpec((None, 1, D_MODEL), lambda i: (row(i), 0, 1)),
                  pl.BlockSpec((1, D_MODEL), lambda i: (0, 0))],
        out_specs=pl.BlockSpec((NORM_TM, D_MODEL), lambda i: (i, 0)),
        out_shape=jax.ShapeDtypeStruct((mc + ml, D_MODEL), BF16),
        compiler_params=_cparams(("arbitrary",)),
        name="norm",
    )(xc, xl, mod, mod, norm_g)


IN_TM = 1024
IN_TN = 512
LORA_TILE = N_RKV // IN_TN
Z_LORA_COL = N_RKV
ZS_COLS = N_RKV + IN_TN


COL_GA = N_SHIFTED
COL_CONV = COL_GA + D_A
COL_MERGE = COL_CONV + 4 * D_B
N_GATES = D_A + 2 * D_MODEL


CONV_TN = 256


def _inproj_conv_kernel(h_ref, wb_ref, wc_ref, wx_ref, wg_ref, cw_ref, o_ref, *, n_ctx_tiles, row_ctx, row_lat):
    i = pl.program_id(1)
    rows = pl.ds(pl.multiple_of(i * IN_TM, IN_TM), IN_TM)
    proj = lambda w_ref: jnp.dot(h_ref[rows, :], w_ref[...].astype(BF16), preferred_element_type=F32)
    p = proj(wc_ref) * proj(wx_ref)
    row_len = jnp.where(i < n_ctx_tiles, row_ctx, row_lat)
    pos = lax.broadcasted_iota(jnp.int32, p.shape, 0) & (row_len - 1)
    pp = jnp.where(pos == 0, 0.0, pltpu.roll(p, 1, 0))
    pn = jnp.where(pos == row_len - 1, 0.0, pltpu.roll(p, IN_TM - 1, 0))
    w = cw_ref[...]
    u = w[0:1] * pp + w[1:2] * p + w[2:3] * pn
    g = proj(wg_ref)
    o_ref[...] = (proj(wb_ref) * u * (g * _sigmoid(g))).astype(BF16)


def _inproj_conv_call(h, w_in, conv_w, m_ctx, row_ctx, row_lat):
    m = h.shape[0]
    assert m % IN_TM == 0 and m_ctx % IN_TM == 0 and IN_TM % row_ctx == 0 and IN_TM % row_lat == 0
    wspec = lambda q: pl.BlockSpec((pl.Element(D_MODEL), pl.Element(CONV_TN)),
                                   lambda j, i: (0, pl.multiple_of(COL_CONV + q * D_B + j * CONV_TN, 128)))
    return pl.pallas_call(
        functools.partial(_inproj_conv_kernel, n_ctx_tiles=m_ctx // IN_TM, row_ctx=row_ctx, row_lat=row_lat),
        grid=(D_B // CONV_TN, m // IN_TM),
        in_specs=[pl.BlockSpec((m, D_MODEL), lambda j, i: (0, 0), pipeline_mode=pl.Buffered(1)),
                  wspec(0), wspec(1), wspec(2), wspec(3),
                  pl.BlockSpec((3, CONV_TN), lambda j, i: (0, j))],
        out_specs=pl.BlockSpec((IN_TM, CONV_TN), lambda j, i: (i, j)),
        out_shape=jax.ShapeDtypeStruct((m, D_B), BF16),
        compiler_params=_cparams(("arbitrary", "arbitrary")),
        name="inproj_conv",
    )(h, w_in, w_in, w_in, w_in, conv_w)


GATE_TN = 256


def _inproj_shift_gates_kernel(h_ref, w_ref, wl_ref, mup_ref, mun_ref, wg_ref, os_ref, og_ref, *,
                               n_shift_steps, n_row_tiles, n_ctx_tiles, seq_ctx, seq_lat):
    t = pl.program_id(0)
    i = t % n_row_tiles
    rows = pl.ds(pl.multiple_of(i * IN_TM, IN_TM), IN_TM)

    def gates():
        og_ref[...] = jnp.dot(h_ref[rows, :], wg_ref[...].astype(BF16), preferred_element_type=F32)

    @pl.when(t < n_shift_steps)
    def _():
        j = t // n_row_tiles
        w = jnp.where(j == LORA_TILE, wl_ref[...], w_ref[...]).astype(BF16)
        z = jnp.dot(h_ref[rows, :], w, preferred_element_type=F32)
        gates()
        mup = mup_ref[...]
        mun = mun_ref[...]
        os_ref[...] = z * (1.0 - mup - mun) + pltpu.roll(z, 1, 0) * mup + pltpu.roll(z, IN_TM - 1, 0) * mun

        def unwrap(seq_len):
            for first in range(0, IN_TM, seq_len):
                last = first + seq_len - 1
                before = (first - 1) % IN_TM
                after = (last + 1) % IN_TM
                os_ref[first:first + 1, :] = os_ref[first:first + 1, :] - mup * z[before:before + 1]
                os_ref[last:last + 1, :] = os_ref[last:last + 1, :] - mun * z[after:after + 1]

        @pl.when(i < n_ctx_tiles)
        def _():
            unwrap(seq_ctx)

        @pl.when(i >= n_ctx_tiles)
        def _():
            unwrap(seq_lat)

    @pl.when(t >= n_shift_steps)
    def _():
        gates()


def _inproj_shift_gates_call(h, w_in, w_lora, mu_prev, mu_next, m_ctx, seq_ctx, seq_lat):
    m = h.shape[0]
    assert m % IN_TM == 0 and m_ctx % IN_TM == 0 and IN_TM % seq_ctx == 0 and IN_TM % seq_lat == 0
    nrt = m // IN_TM
    n_shift = (LORA_TILE + 1) * nrt
    n_gate_tiles = N_GATES // GATE_TN
    n_steps = n_gate_tiles * nrt
    assert n_steps >= n_shift
    ts = lambda t: jnp.minimum(t, n_shift - 1)
    w_col = lambda t: pl.multiple_of(jnp.minimum(ts(t) // nrt, LORA_TILE - 1) * IN_TN, 128)
    n_ga = D_A // GATE_TN
    g_col = lambda t: pl.multiple_of(jnp.where(t // nrt < n_ga, COL_GA + (t // nrt) * GATE_TN,
                                               COL_MERGE + (t // nrt - n_ga) * GATE_TN), 128)
    resident = dict(pipeline_mode=pl.Buffered(1))
    return pl.pallas_call(
        functools.partial(_inproj_shift_gates_kernel, n_shift_steps=n_shift, n_row_tiles=nrt,
                          n_ctx_tiles=m_ctx // IN_TM, seq_ctx=seq_ctx, seq_lat=seq_lat),
        grid=(n_steps,),
        in_specs=[pl.BlockSpec((m, D_MODEL), lambda t: (0, 0), **resident),
                  pl.BlockSpec((pl.Element(D_MODEL), pl.Element(IN_TN)), lambda t: (0, w_col(t))),
                  pl.BlockSpec((D_MODEL, IN_TN), lambda t: (0, 0), **resident),
                  pl.BlockSpec((1, IN_TN), lambda t: (0, ts(t) // nrt)),
                  pl.BlockSpec((1, IN_TN), lambda t: (0, ts(t) // nrt)),
                  pl.BlockSpec((pl.Element(D_MODEL), pl.Element(GATE_TN)), lambda t: (0, g_col(t)))],
        out_specs=[pl.BlockSpec((IN_TM, IN_TN), lambda t: (ts(t) % nrt, ts(t) // nrt)),
                   pl.BlockSpec((IN_TM, GATE_TN), lambda t: (t % nrt, t // nrt))],
        out_shape=[jax.ShapeDtypeStruct((m, ZS_COLS), F32), jax.ShapeDtypeStruct((m, N_GATES), F32)],
        compiler_params=_cparams(("arbitrary",)),
        name="inproj_shift_gates",
    )(h, w_in, w_lora, mu_prev, mu_next, w_in)


WKV_BLK = 1024
N_STAGE = 4
STAGE_CHUNKS = WKV_BLK // CHUNK // N_STAGE


def _lockstep(tasks):
    tasks = list(tasks)
    while tasks:
        alive = []
        for t in tasks:
            try:
                next(t)
                alive.append(t)
            except StopIteration:
                pass
        tasks = alive


def _wkv_kernel(*refs, seq_len, has_s0, emit_state):
    it = iter(refs)
    r_ref, k_ref, v_ref, lora_ref, ga_ref, p8_ref, w0a0_ref, w2a2_ref = (next(it) for _ in range(8))
    s0_refs = (next(it), next(it)) if has_s0 else None
    ya_ref = next(it)
    st_refs = (next(it), next(it)) if emit_state else None
    (atrt_ref, arb_ref, bkt_ref, u0_ref, y0_ref, pct_ref, kk_ref, lw_ref, kd_ref, bb_ref,
     yacc_ref, bonus_ref, s_ref) = (next(it) for _ in range(13))

    n_seq = WKV_BLK // seq_len
    n_chunks = WKV_BLK // CHUNK
    p8 = p8_ref[...]
    k_k, k_a, r_k, lnx_g, lnx_b = (p8[i:i + 1] for i in range(5))
    w0a0 = w0a0_ref[...]

    row_c = lax.broadcasted_iota(jnp.int32, (CHUNK, GC), 0)
    col_c = lax.broadcasted_iota(jnp.int32, (CHUNK, GC), 1) & (HEAD - 1)
    strict = (col_c < row_c, col_c > row_c)
    incl = (col_c <= row_c, col_c >= row_c)
    eye_pk = jnp.where(col_c == row_c, 1.0, 0.0)
    same_blk = {}
    m = INV_BASE
    while m <= CHUNK:
        same_blk[m] = (row_c // m) == (col_c // m)
        m *= 2
    near = [{k: same_blk[INV_BASE] & (sg * (row_c - col_c) == k) for k in (2, 3)} for sg in (1, -1)]
    rb = lax.broadcasted_iota(jnp.int32, (GC, GC), 0) // HEAD
    cb = lax.broadcasted_iota(jnp.int32, (GC, GC), 1) // HEAD
    bd_mask = rb == cb
    ones_bd16 = jnp.where(bd_mask, 1.0, 0.0).astype(BF16)

    def bd16(x):
        t = jnp.concatenate([x.astype(F32)] * G_HEADS, axis=0)
        return jnp.where(bd_mask, t, 0.0).astype(BF16)

    def split2(x):
        h1 = x.astype(BF16)
        return h1, (x - h1.astype(F32)).astype(BF16)

    def ones_dot(x16):
        return jnp.dot(x16, ones_bd16, preferred_element_type=F32)

    def running_sum(x, d):
        s = 1
        while s < CHUNK:
            if d == 0:
                x = x + jnp.where(row_c >= s, pltpu.roll(x, s, 0), 0.0)
            else:
                x = x + jnp.where(row_c < CHUNK - s, pltpu.roll(x, CHUNK - s, 0), 0.0)
            s *= 2
        return x

    def prep(j):
        rows = pl.ds(j * WKV_ROWS, WKV_ROWS)
        r = r_ref[rows, :]
        k = k_ref[rows, :]
        lo = lora_ref[rows, :]
        kk = k * k_k
        q1, q2 = split2(kk * kk)
        ssq = ones_dot(q1)
        yield
        ssq = ssq + ones_dot(q2)
        yield
        kk = kk / jnp.maximum(jnp.sqrt(ssq), 1e-12)
        kk_ref[rows, :] = kk
        kd_sum = None
        for d in range(N_DIR):
            wd = jnp.tanh(lo[:, d * R_PAD:(d + 1) * R_PAD])
            ad = lo[:, (N_DIR + d) * R_PAD:(N_DIR + d + 1) * R_PAD]
            w_lin = w0a0[d:d + 1] + _bdot(wd, w2a2_ref[d])
            yield
            lw_ref[d, rows, :] = -EXP_M05 * _sigmoid(w_lin)
            a = _sigmoid(w0a0[N_DIR + d:N_DIR + d + 1] + _bdot(ad, w2a2_ref[N_DIR + d]))
            yield
            kd = k * (1.0 + (a - 1.0) * k_a)
            kd_ref[d, rows, :] = kd
            bb_ref[d, rows, :] = kk * a
            kd_sum = kd if kd_sum is None else kd_sum + kd
        b1, b2 = split2(r * kd_sum * r_k)
        bs = ones_dot(b1)
        yield
        bs = bs + ones_dot(b2)
        yield
        bonus_ref[rows, :] = bs * v_ref[rows, :]

    def a_job(c, d):
        rows = pl.ds(c * CHUNK, CHUNK)
        r = r_ref[rows, :]
        kk = kk_ref[rows, :]
        lw = lw_ref[d, rows, :]
        cs = running_sum(lw, d)
        e = jnp.exp(cs)
        e_inv = jnp.exp(-cs)
        rt = r * e
        at = -kk * jnp.exp(cs - lw)
        bt = bb_ref[d, rows, :] * e_inv
        kt = kd_ref[d, rows, :] * e_inv
        atrt_ref[d, c, CHUNK:2 * CHUNK, :] = rt.astype(BF16)
        bkt_ref[d, c] = jnp.concatenate([bt, kt], axis=0).T.astype(BF16)
        edge = e[CHUNK - 1:CHUNK] if d == 0 else e[0:1]
        pct_ref[d, c] = jnp.broadcast_to(edge, (GC // 2, GC)).T
        ar = jnp.concatenate([at, rt], axis=0).astype(BF16)
        sb = _bdot_nt(ar, bd16(bt))
        yield
        sk = _bdot_nt(ar, bd16(kt))
        yield
        l_ab = jnp.where(strict[d], sb[:CHUNK], 0.0)
        arb_ref[d, c] = jnp.where(incl[d], sb[CHUNK:], 0.0).astype(BF16)
        lak_ark = jnp.concatenate([jnp.where(strict[d], sk[:CHUNK], 0.0),
                                   jnp.where(incl[d], sk[CHUNK:], 0.0)], axis=0).astype(BF16)
        sgn = 1 if d == 0 else -1
        l4 = jnp.where(same_blk[INV_BASE], l_ab, 0.0)
        a1 = pltpu.roll(l4, (-sgn) % GC, 1)
        a2 = pltpu.roll(l4, (-2 * sgn) % GC, 1)
        t2 = l4 + jnp.where(near[d][2], a1 * pltpu.roll(l4, sgn % CHUNK, 0), 0.0)
        t3 = a1 * pltpu.roll(l4, (2 * sgn) % CHUNK, 0) + a2 * pltpu.roll(t2, sgn % CHUNK, 0)
        tm = eye_pk + t2 + jnp.where(near[d][3], t3, 0.0)
        m = INV_BASE
        while m < CHUNK:
            off = same_blk[2 * m] & jnp.logical_not(same_blk[m])
            m1 = _bdot(tm, bd16(jnp.where(off, l_ab, 0.0)))
            yield
            tm = tm + _bdot(m1, bd16(tm))
            yield
            m *= 2
        atrt_ref[d, c, 0:CHUNK, :] = _bdot(tm, bd16(at)).astype(BF16)
        yield
        xv = _bdot(lak_ark, bd16(v_ref[rows, :]))
        yield
        y0_ref[d, c] = xv[CHUNK:]
        u0_ref[d, c] = _bdot(tm, bd16(xv[:CHUNK]))
        yield

    def b_chain(d, chunks, init, out_seq):
        if init == "zero":
            s_ref[d] = jnp.zeros((GC, GC), F32)
        elif init == "given":
            tile16 = jnp.where(lax.broadcasted_iota(jnp.int32, (HEAD, GC), 0)
                               == (lax.broadcasted_iota(jnp.int32, (HEAD, GC), 1) & (HEAD - 1)),
                               1.0, 0.0).astype(BF16)
            s_ref[d] = jnp.where(bd_mask, _dot_exact_rhs(s0_refs[d][0], tile16), 0.0).T
            yield
        for c in chunks:
            rows = pl.ds(c * CHUNK, CHUNK)
            s = s_ref[d]
            us = _bdot(atrt_ref[d, c], s.astype(BF16))
            yield
            u = us[:CHUNK] + u0_ref[d, c]
            uv = jnp.concatenate([u.astype(BF16), v_ref[rows, :].astype(BF16)], axis=0)
            ds = _bdot(bkt_ref[d, c], uv)
            yield
            pcc = pct_ref[d, c]
            s_ref[d] = (s + jnp.where(bd_mask, ds, 0.0)) * jnp.concatenate([pcc, pcc], axis=1)
            y = us[CHUNK:] + _bdot(arb_ref[d, c], bd16(u)) + y0_ref[d, c]
            yield
            yacc_ref[rows, :] += y
        if out_seq is not None:
            untile16 = jnp.where((lax.broadcasted_iota(jnp.int32, (GC, HEAD), 0) & (HEAD - 1))
                                 == lax.broadcasted_iota(jnp.int32, (GC, HEAD), 1), 1.0, 0.0).astype(BF16)
            st_refs[d][out_seq] = _dot_exact_rhs(s_ref[d].T, untile16)
            yield

    def post(j):
        rows = pl.ds(j * WKV_ROWS, WKV_ROWS)
        y = yacc_ref[rows, :]
        y1, y2 = split2(y)
        mu = ones_dot(y1)
        yield
        mu = (mu + ones_dot(y2)) * (1.0 / HEAD)
        yield
        yc = y - mu
        v1, v2 = split2(yc * yc)
        var = ones_dot(v1)
        yield
        var = (var + ones_dot(v2)) * (1.0 / HEAD)
        yield
        out = yc * lax.rsqrt(var + GN_EPS) * lnx_g + lnx_b + bonus_ref[rows, :]
        ga = ga_ref[rows, :]
        ya_ref[rows, :] = (out * (ga * _sigmoid(ga))).astype(BF16)

    yacc_ref[...] = jnp.zeros_like(yacc_ref)
    sc = STAGE_CHUNKS
    if n_seq == 1:
        fwd = [[i * sc + q for q in range(sc)] for i in range(N_STAGE)]
        bwd = [[n_chunks - 1 - i * sc - q for q in range(sc)] for i in range(N_STAGE)]
        a_jobs = [[a_job(c, 0) for c in fwd[i]] + [a_job(c, 1) for c in bwd[i]] for i in range(N_STAGE)]
        init = lambda i: ("given" if has_s0 else "zero") if i == 0 else None
        chains = [[b_chain(0, fwd[i], init(i), None), b_chain(1, bwd[i], init(i), None)] for i in range(N_STAGE)]
        stages = [
            ([], [prep(0), prep(N_STAGE - 1)]),
            (a_jobs[0], [prep(j) for j in range(1, N_STAGE - 1)]),
        ]
        for i in range(1, N_STAGE):
            stages.append((a_jobs[i], chains[i - 1]))
        stages.append(([], chains[N_STAGE - 1] + [post(j) for j in range(1, N_STAGE - 1)]))
        stages.append(([], [post(0), post(N_STAGE - 1)]))
    else:
        assert n_seq == N_STAGE and seq_len == sc * CHUNK
        fwd = [[i * sc + q for q in range(sc)] for i in range(N_STAGE)]
        bwd = [list(reversed(f)) for f in fwd]
        a_jobs = [[a_job(c, 0) for c in fwd[i]] + [a_job(c, 1) for c in bwd[i]] for i in range(N_STAGE)]
        out_seq = (lambda i: i) if emit_state else (lambda i: None)
        chains = [[b_chain(0, fwd[i], "zero", out_seq(i)), b_chain(1, bwd[i], "zero", out_seq(i))]
                  for i in range(N_STAGE)]
        stages = [([], [prep(0)]), (a_jobs[0], [prep(1)])]
        for i in range(1, N_STAGE):
            side = list(chains[i - 1])
            if i + 1 < N_STAGE:
                side.append(prep(i + 1))
            if i >= 2:
                side.append(post(i - 2))
            stages.append((a_jobs[i], side))
        stages.append(([], chains[N_STAGE - 1] + [post(N_STAGE - 2)]))
        stages.append(([], [post(N_STAGE - 1)]))
    for jobs, side in stages:
        _lockstep(jobs + side)


def _wkv_call(zs, z, row0, m, p8, w0a0, w2a2, seq_len, s0=None, emit_state=False, name="wkv"):
    assert row0 % WKV_BLK == 0 and m % WKV_BLK == 0 and WKV_BLK % seq_len == 0
    n_seq = WKV_BLK // seq_len
    nb = m // WKV_BLK
    b0 = row0 // WKV_BLK
    nc = WKV_BLK // CHUNK
    has_s0 = s0 is not None
    assert not has_s0 or n_seq == 1
    ncb = D_A // GC
    lora_w = 2 * N_DIR * R_PAD
    in_specs = [
        pl.BlockSpec((WKV_BLK, GC), lambda b, g: (b0 + b, g)),
        pl.BlockSpec((WKV_BLK, GC), lambda b, g: (b0 + b, ncb + g)),
        pl.BlockSpec((WKV_BLK, GC), lambda b, g: (b0 + b, 2 * ncb + g)),
        pl.BlockSpec((WKV_BLK, lora_w), lambda b, g: (b0 + b, Z_LORA_COL // lora_w)),
        pl.BlockSpec((WKV_BLK, GC), lambda b, g: (b0 + b, g)),
        pl.BlockSpec((8, GC), lambda b, g: (0, g)),
        pl.BlockSpec((2 * N_DIR, GC), lambda b, g: (0, g)),
        pl.BlockSpec((2 * N_DIR, R_PAD, GC), lambda b, g: (0, 0, g)),
    ]
    args = [zs, zs, zs, zs, z, p8, w0a0, w2a2]
    if has_s0:
        in_specs += [pl.BlockSpec((1, GC, HEAD), lambda b, g: (b, g, 0))] * 2
        args += list(s0)
    out_specs = [pl.BlockSpec((WKV_BLK, GC), lambda b, g: (b, g))]
    out_shape = [jax.ShapeDtypeStruct((m, D_A), BF16)]
    if emit_state:
        out_specs += [pl.BlockSpec((n_seq, GC, HEAD), lambda b, g: (b, g, 0))] * 2
        out_shape += [jax.ShapeDtypeStruct((nb * n_seq, D_A, HEAD), F32)] * 2
    pair16 = pltpu.VMEM((N_DIR, nc, 2 * CHUNK, GC), BF16)
    big16 = pltpu.VMEM((N_DIR, nc, CHUNK, GC), BF16)
    big32 = pltpu.VMEM((N_DIR, nc, CHUNK, GC), F32)
    seq32 = pltpu.VMEM((WKV_BLK, GC), F32)
    dir32 = pltpu.VMEM((N_DIR, WKV_BLK, GC), F32)
    scratch = [pair16, big16, pltpu.VMEM((N_DIR, nc, GC, 2 * CHUNK), BF16), big32, big32,
               pltpu.VMEM((N_DIR, nc, GC, GC // 2), F32),
               seq32, dir32, dir32, dir32,
               seq32, seq32,
               pltpu.VMEM((N_DIR, GC, GC), F32)]
    return pl.pallas_call(
        functools.partial(_wkv_kernel, seq_len=seq_len, has_s0=has_s0, emit_state=emit_state),
        grid=(nb, N_GROUPS),
        in_specs=in_specs,
        out_specs=out_specs,
        out_shape=out_shape,
        scratch_shapes=scratch,
        compiler_params=_cparams(("arbitrary", "arbitrary")),
        name=name,
    )(*args)


OUT_TM = 256


def _outproj_kernel(ya_ref, yb_ref, ma_ref, mb_ref, x_ref, gate_ref, fg_ref, wa_ref, wb_ref, wo_ref, o_ref):
    y_a = jnp.dot(ya_ref[...], wa_ref[...], preferred_element_type=F32)
    y_b = jnp.dot(yb_ref[...], wb_ref[...], preferred_element_type=F32)
    merged = _sigmoid(ma_ref[...]) * y_a + _sigmoid(mb_ref[...]) * y_b
    out = jnp.dot(merged.astype(BF16), wo_ref[...], preferred_element_type=F32)
    xo = x_ref[...] + gate_ref[...] * out
    ms = jnp.mean(xo * xo, axis=-1, keepdims=True)
    o_ref[...] = xo * lax.rsqrt(ms + RMS_EPS) * fg_ref[...]


def _outproj_call(ya, yb, z, row0, x, mod, final_g, wa16, wb16, wo16, rows_per_mod, mod_row0, mod_row_step,
                  name="outproj"):
    m = x.shape[0]
    mcol = D_A // D_MODEL
    i0 = row0 // OUT_TM
    row = lambda i: mod_row0 + mod_row_step * (i // rows_per_mod)
    full = lambda: pl.BlockSpec((D_MODEL, D_MODEL), lambda i: (0, 0), pipeline_mode=pl.Buffered(1))
    tile = lambda col: pl.BlockSpec((OUT_TM, D_MODEL), lambda i: (i, col))
    ztile = lambda col: pl.BlockSpec((OUT_TM, D_MODEL), lambda i: (i0 + i, col))
    return pl.pallas_call(
        _outproj_kernel,
        grid=(m // OUT_TM,),
        in_specs=[tile(0), ztile(0), ztile(mcol), ztile(mcol + 1), tile(0),
                  pl.BlockSpec((None, 1, D_MODEL), lambda i: (row(i), 0, 2)),
                  pl.BlockSpec((1, D_MODEL), lambda i: (0, 0)),
                  full(), full(), full()],
        out_specs=tile(0),
        out_shape=jax.ShapeDtypeStruct((m, D_MODEL), F32),
        compiler_params=_cparams(("arbitrary",)),
        name=name,
    )(ya, yb, z, z, x, mod, final_g, wa16, wb16, wo16)


def _pad_lora(a, axis):
    shp = a.shape
    a = a.reshape(shp[:axis] + (2 * N_DIR, R_LORA) + shp[axis + 1:])
    pad = [(0, 0)] * a.ndim
    pad[axis + 1] = (0, R_PAD - R_LORA)
    a = jnp.pad(a, pad)
    return a.reshape(shp[:axis] + (2 * N_DIR * R_PAD,) + shp[axis + 1:])


def _stream(x, zs, z, yb, row0, seq_len, mod, mod_row0, mod_row_step, lw, s0, emit_state, tag):
    (p8, w0a0, w2a2, wa16, wb16, wo16, final_g) = lw
    m = x.shape[0]
    wkv_out = _wkv_call(zs, z, row0, m, p8, w0a0, w2a2, seq_len, s0=s0, emit_state=emit_state, name="wkv_" + tag)
    ya = wkv_out[0]
    y = _outproj_call(ya, yb, z, row0, x, mod, final_g, wa16, wb16, wo16, seq_len // OUT_TM,
                      mod_row0, mod_row_step, name="outproj_" + tag)
    return y, wkv_out[1:]


def kernel(x_prompt, x_sample, state_wkv_fwd, state_wkv_bwd, c, c_ctx, ada_w, ada_b, norm_g, w_in, mu_prev,
           mu_next, w0, w2, a0, a2, k_k, k_a, r_k, lnx_g, lnx_b, conv_w, w_out_a, w_out_b, w_o, final_g):
    depth = norm_g.shape[0]
    assert depth == 1
    l = 0
    nbp, seq_p, _ = x_prompt.shape
    nbs, seq_s, _ = x_sample.shape
    lat_rows = seq_s // GRID_W
    xc = x_prompt.reshape(nbp * seq_p, D_MODEL)
    xl = x_sample.reshape(nbs * seq_s, D_MODEL)

    cvec = jnp.concatenate([c, c_ctx[None, :], jnp.zeros((8 - nbs - 1, D_MODEL), F32)], axis=0)
    mod = _ada_call(cvec, ada_w[l], ada_b[l][None, :]).reshape(8, 1, 3 * D_MODEL)

    w_lora = _pad_lora(w_in[l][:, N_RKV:N_SHIFTED], 1)
    mu_p = jnp.concatenate([mu_prev[l][None, :N_RKV], _pad_lora(mu_prev[l][None, N_RKV:], 1)], axis=1)
    mu_n = jnp.concatenate([mu_next[l][None, :N_RKV], _pad_lora(mu_next[l][None, N_RKV:], 1)], axis=1)
    zrow = jnp.zeros((D_A,), F32)
    p8 = jnp.stack([k_k[l], k_a[l], r_k[l].reshape(D_A), lnx_g[l], lnx_b[l], zrow, zrow, zrow], axis=0)
    w0a0 = jnp.concatenate([w0[l], a0[l]], axis=0)
    w2a2 = jnp.pad(jnp.concatenate([w2[l], a2[l]], axis=0), ((0, 0), (0, R_PAD - R_LORA), (0, 0)))
    lw = (p8, w0a0, w2a2, w_out_a[l].astype(BF16), w_out_b[l].astype(BF16), w_o[l].astype(BF16), final_g[None, :])

    h = _norm_call(xc, xl, mod, norm_g[l][None, :], seq_s, nbs)
    zs, z = _inproj_shift_gates_call(h, w_in[l], w_lora, mu_p, mu_n, xc.shape[0], seq_p, seq_s)
    yb = _inproj_conv_call(h, w_in[l], conv_w[l], xc.shape[0], seq_p, seq_s // lat_rows)

    y_prompt, st = _stream(xc, zs, z, yb, 0, seq_p, mod, nbs, 0, lw, None, True, "ctx")
    s0 = (state_wkv_fwd[:, l].reshape(nbs, D_A, HEAD), state_wkv_bwd[:, l].reshape(nbs, D_A, HEAD))
    y_sample, _ = _stream(xl, zs, z, yb, xc.shape[0], seq_s, mod, 0, 1, lw, s0, False, "lat")
    new_f = st[0].reshape(nbp, 1, N_HEADS, HEAD, HEAD)
    new_b = st[1].reshape(nbp, 1, N_HEADS, HEAD, HEAD)
    return (y_prompt.reshape(nbp, seq_p, D_MODEL), y_sample.reshape(nbs, seq_s, D_MODEL), new_f, new_b)
```
